```python
import math
import jax, jax.numpy as jnp
from jax import lax
import numpy as np

D_MODEL = 2048
BATCH = 16
SEQ = 2048
DEPTH = 2

MIX_WIDTH = D_MODEL
N_MLSTM_HEADS = 4
MLSTM_HEAD_DIM = MIX_WIDTH // 2 // N_MLSTM_HEADS
MLSTM_WIDTH = N_MLSTM_HEADS * MLSTM_HEAD_DIM
N_RET_HEADS = 4
RET_HEAD_DIM = MIX_WIDTH // 2 // N_RET_HEADS
RET_WIDTH = N_RET_HEADS * RET_HEAD_DIM
CONV_WIDTH = 4
GATE_SOFTCAP = 15.0
CHUNK = 128
ROPE_BASE = 10000.0
EVEN_IN_WIDTH = 4 * MLSTM_WIDTH + 2 * N_MLSTM_HEADS + 4 * RET_WIDTH
N_ATTN_HEADS = 16
ATTN_HEAD_DIM = MIX_WIDTH // N_ATTN_HEADS
N_IDX_HEADS = 16
IDX_HEAD_DIM = 64
MAX_TOPK = 256
Q_BLOCK = 128
ODD_IN_WIDTH = (N_ATTN_HEADS * ATTN_HEAD_DIM + 2 * ATTN_HEAD_DIM
                + N_IDX_HEADS * IDX_HEAD_DIM + IDX_HEAD_DIM + N_IDX_HEADS)
REL_BUCKETS = 32
REL_MAX_DISTANCE = 128
D_FF = 4 * D_MODEL
N_EVEN = (DEPTH + 1) // 2
N_ODD = DEPTH // 2
EPS = 1e-6

kernel_name = "hybrid_mlstm_retention_dsa_trunk"


def rms_norm(x, g):
    xf = x.astype(jnp.float32)
    y = xf * lax.rsqrt(jnp.mean(xf * xf, axis=-1, keepdims=True) + EPS)
    return (y * g.astype(jnp.float32)).astype(x.dtype)


def causal_short_conv(x, w):
    k_w = w.shape[0]
    s = x.shape[1]
    xp = jnp.pad(x, ((0, 0), (k_w - 1, 0), (0, 0)))
    return sum(xp[:, j:j + s] * w[j] for j in range(k_w))


def rotary(x, pos):
    half = x.shape[-1] // 2
    inv = ROPE_BASE ** (-jnp.arange(half, dtype=jnp.float32) / half)
    ang = pos.astype(jnp.float32)[:, None] * inv[None, :]
    cos = jnp.cos(ang)[None, :, None, :]
    sin = jnp.sin(ang)[None, :, None, :]
    x1, x2 = x[..., :half], x[..., half:]
    return jnp.concatenate([x1 * cos - x2 * sin, x2 * cos + x1 * sin], axis=-1)


def to_chunks(t):
    b, h, s = t.shape[:3]
    t = t.reshape(b, h, s // CHUNK, CHUNK, *t.shape[3:])
    return jnp.moveaxis(t, 2, 0)


def from_chunks(t):
    nc, b, h, l, d = t.shape
    return jnp.moveaxis(t, 0, 2).reshape(b, h, nc * l, d)


def mlstm_chunkwise(q, k, v, log_i, log_f):
    b, h, s, dk = q.shape
    dv = v.shape[-1]
    causal = jnp.tril(jnp.ones((CHUNK, CHUNK), dtype=bool))

    def step(carry, xs):
        c_st, n_st, m_prev = carry
        qc, kc, vc, lic, lfc = xs
        bcum = jnp.cumsum(lfc, axis=-1)
        dmat = bcum[..., :, None] - bcum[..., None, :] + lic[..., None, :]
        dmat = jnp.where(causal, dmat, -jnp.inf)
        m_inter = bcum + m_prev[..., None]
        m_row = jnp.maximum(jnp.max(dmat, axis=-1), m_inter)
        sc = jnp.einsum('bhjd,bhsd->bhjs', qc, kc) * jnp.exp(dmat - m_row[..., None])
        inter = jnp.exp(m_inter - m_row)
        num = (jnp.einsum('bhjs,bhse->bhje', sc, vc)
               + inter[..., None] * jnp.einsum('bhjd,bhde->bhje', qc, c_st))
        den = sc.sum(-1) + inter * jnp.einsum('bhjd,bhd->bhj', qc, n_st)
        h_out = num / jnp.maximum(jnp.abs(den), jnp.exp(-m_row))[..., None]
        b_last = bcum[..., -1]
        g = b_last[..., None] - bcum + lic
        m_new = jnp.maximum(b_last + m_prev, jnp.max(g, axis=-1))
        w = jnp.exp(g - m_new[..., None])
        decay = jnp.exp(b_last + m_prev - m_new)
        c_new = decay[..., None, None] * c_st + jnp.einsum('bhsd,bhse->bhde', kc * w[..., None], vc)
        n_new = decay[..., None] * n_st + jnp.einsum('bhs,bhsd->bhd', w, kc)
        return (c_new, n_new, m_new), h_out

    init = (jnp.zeros((b, h, dk, dv), jnp.float32), jnp.zeros((b, h, dk), jnp.float32),
            jnp.zeros((b, h), jnp.float32))
    _, hs = lax.scan(step, init, (to_chunks(q), to_chunks(k), to_chunks(v),
                                  to_chunks(log_i), to_chunks(log_f)))
    return from_chunks(hs)


def retention_chunkwise(q, k, v, log_gamma):
    b, h, s, dk = q.shape
    dv = v.shape[-1]
    j = jnp.arange(CHUNK, dtype=jnp.float32)
    rel = j[:, None] - j[None, :]
    dmask = jnp.where(rel >= 0, jnp.exp(jnp.maximum(rel, 0.0)[None] * log_gamma[:, None, None]), 0.0)
    cross_decay = jnp.exp((j + 1.0)[None, :] * log_gamma[:, None])
    state_decay = jnp.exp((CHUNK - 1.0 - j)[None, :] * log_gamma[:, None])
    chunk_decay = jnp.exp(CHUNK * log_gamma)

    def step(r_st, xs):
        qc, kc, vc = xs
        inner = jnp.einsum('bhjs,bhse->bhje', jnp.einsum('bhjd,bhsd->bhjs', qc, kc) * dmask, vc)
        cross = jnp.einsum('bhjd,bhde->bhje', qc, r_st) * cross_decay[..., None]
        r_new = chunk_decay[:, None, None] * r_st + jnp.einsum(
            'bhsd,bhse->bhde', kc * state_decay[..., None], vc)
        return r_new, inner + cross

    init = jnp.zeros((b, h, dk, dv), jnp.float32)
    _, hs = lax.scan(step, init, (to_chunks(q), to_chunks(k), to_chunks(v)))
    return from_chunks(hs)


def mlstm_retention_mixer(h, w_in, conv_w, gate_b, head_norm_g, w_out):
    bsz, s, _ = h.shape
    dt = h.dtype
    p = h @ w_in
    cuts = np.cumsum([2 * MLSTM_WIDTH, MLSTM_WIDTH, MLSTM_WIDTH, 2 * N_MLSTM_HEADS,
                      RET_WIDTH, RET_WIDTH, RET_WIDTH]).tolist()
    qk_m, v_m, o_m, if_m, q_r, k_r, v_r, g_r = jnp.split(p, cuts, axis=-1)
    qk_m = jax.nn.silu(causal_short_conv(qk_m, conv_w))
    q_m, k_m = jnp.split(qk_m, 2, axis=-1)
    gates = if_m.astype(jnp.float32) + gate_b.astype(jnp.float32)
    gates = GATE_SOFTCAP * jnp.tanh(gates / GATE_SOFTCAP)
    log_i = jnp.transpose(gates[..., :N_MLSTM_HEADS], (0, 2, 1))
    log_f = jnp.transpose(jax.nn.log_sigmoid(gates[..., N_MLSTM_HEADS:]), (0, 2, 1))

    def heads(t, nh):
        return jnp.transpose(t.reshape(bsz, s, nh, -1), (0, 2, 1, 3)).astype(jnp.float32)

    h_m = mlstm_chunkwise(heads(q_m, N_MLSTM_HEADS),
                          heads(k_m, N_MLSTM_HEADS) * (MLSTM_HEAD_DIM ** -0.5),
                          heads(v_m, N_MLSTM_HEADS), log_i, log_f)
    pos = jnp.arange(s)
    qr = rotary(q_r.reshape(bsz, s, N_RET_HEADS, RET_HEAD_DIM).astype(jnp.float32), pos)
    kr = rotary(k_r.reshape(bsz, s, N_RET_HEADS, RET_HEAD_DIM).astype(jnp.float32), pos)
    log_gamma = jnp.log(1.0 - 2.0 ** (-5.0 - jnp.arange(N_RET_HEADS, dtype=jnp.float32)))
    h_r = retention_chunkwise(jnp.transpose(qr, (0, 2, 1, 3)),
                              jnp.transpose(kr, (0, 2, 1, 3)) * (RET_HEAD_DIM ** -0.5),
                              heads(v_r, N_RET_HEADS), log_gamma)
    y = jnp.concatenate([h_m, h_r], axis=1)
    y = y * lax.rsqrt(jnp.mean(y * y, axis=-1, keepdims=True) + EPS)
    y = y * head_norm_g.astype(jnp.float32)[None, :, None, :]
    y = jnp.transpose(y, (0, 2, 1, 3)).reshape(bsz, s, MIX_WIDTH)
    gate = jnp.concatenate([jax.nn.sigmoid(o_m.astype(jnp.float32)),
                            jax.nn.silu(g_r.astype(jnp.float32))], axis=-1)
    return (y * gate).astype(dt) @ w_out


def t5_bucket(dist):
    max_exact = REL_BUCKETS // 2
    d = jnp.maximum(dist, 0)
    large = max_exact + (jnp.log(jnp.maximum(d, 1).astype(jnp.float32) / max_exact)
                         / math.log(REL_MAX_DISTANCE / max_exact)
                         * (REL_BUCKETS - max_exact)).astype(jnp.int32)
    large = jnp.minimum(large, REL_BUCKETS - 1)
    return jnp.where(d < max_exact, d, large)


def dsa_mixer(h, w_in, q_norm_g, k_norm_g, rel_bias, w_out):
    bsz, s, _ = h.shape
    dt = h.dtype
    p = h @ w_in
    cuts = np.cumsum([N_ATTN_HEADS * ATTN_HEAD_DIM, ATTN_HEAD_DIM, ATTN_HEAD_DIM,
                      N_IDX_HEADS * IDX_HEAD_DIM, IDX_HEAD_DIM]).tolist()
    q, k_sh, v_sh, iq, ik, iw = jnp.split(p, cuts, axis=-1)
    q = rms_norm(q.reshape(bsz, s, N_ATTN_HEADS, ATTN_HEAD_DIM), q_norm_g).astype(jnp.float32)
    k_sh = rms_norm(k_sh, k_norm_g)
    kv = jnp.concatenate([k_sh, v_sh], axis=-1).astype(jnp.float32)
    iq = iq.reshape(bsz, s, N_IDX_HEADS, IDX_HEAD_DIM).astype(jnp.float32)
    ik = ik.astype(jnp.float32)
    iw = iw.astype(jnp.float32) * (N_IDX_HEADS ** -0.5)
    topk = min(MAX_TOPK, s // 4)
    nb = s // Q_BLOCK
    key_pos = jnp.arange(s)

    def blocks(t):
        return jnp.moveaxis(t.reshape(bsz, nb, Q_BLOCK, *t.shape[2:]), 1, 0)

    def attend_block(xs):
        qb, iqb, iwb, qpos = xs
        sc = jax.nn.relu(jnp.einsum('bqhd,bkd->bqhk', iqb, ik)) * (IDX_HEAD_DIM ** -0.5)
        sc = jnp.einsum('bqhk,bqh->bqk', sc, iwb)
        sc = jnp.where(key_pos[None, None, :] <= qpos[None, :, None], sc, -jnp.inf)
        _, idx = lax.top_k(sc, topk)
        valid = idx <= qpos[None, :, None]
        sel = jax.vmap(lambda a, i: a[i])(kv, idx)
        k_sel, v_sel = sel[..., :ATTN_HEAD_DIM], sel[..., ATTN_HEAD_DIM:]
        logits = jnp.einsum('bqhd,bqkd->bqhk', qb, k_sel) * (ATTN_HEAD_DIM ** -0.5)
        bias = rel_bias.astype(jnp.float32)[t5_bucket(qpos[None, :, None] - idx)]
        logits = logits + jnp.moveaxis(bias, -1, 2)
        logits = jnp.where(valid[:, :, None, :], logits, -jnp.inf)
        probs = jax.nn.softmax(logits, axis=-1)
        return jnp.einsum('bqhk,bqkd->bqhd', probs, v_sel)

    out = lax.map(attend_block, (blocks(q), blocks(iq), blocks(iw), key_pos.reshape(nb, Q_BLOCK)))
    out = jnp.moveaxis(out, 0, 1).reshape(bsz, s, N_ATTN_HEADS * ATTN_HEAD_DIM)
    return out.astype(dt) @ w_out


def sqrelu_mlp(h, w1, w2):
    return jnp.square(jax.nn.relu(h @ w1)) @ w2


def setup_inputs(seed: int = 0) -> dict:
    key = jax.random.key(seed)
    ks = jax.random.split(key, 20)
    f32 = jnp.float32
    nrm = lambda k, shp: jax.random.normal(k, shp, f32)
    gate_b = jnp.concatenate([
        0.1 * nrm(ks[10], (N_EVEN, N_MLSTM_HEADS)),
        jnp.linspace(3.0, 6.0, N_MLSTM_HEADS, dtype=f32)[None, :] + 0.1 * nrm(ks[11], (N_EVEN, N_MLSTM_HEADS)),
    ], axis=-1)
    return {
        "x": nrm(ks[0], (BATCH, SEQ, D_MODEL)),
        "c": nrm(ks[1], (BATCH, D_MODEL)),
        "ada_w": nrm(ks[2], (DEPTH, D_MODEL, 6 * D_MODEL)) * D_MODEL ** -0.5,
        "ada_b": 0.01 * nrm(ks[3], (DEPTH, 6 * D_MODEL)),
        "norm1_g": 1.0 + 0.02 * nrm(ks[4], (DEPTH, D_MODEL)),
        "norm2_g": 1.0 + 0.02 * nrm(ks[5], (DEPTH, D_MODEL)),
        "mlp_w1": nrm(ks[6], (DEPTH, D_MODEL, D_FF)) * D_MODEL ** -0.5,
        "mlp_w2": nrm(ks[7], (DEPTH, D_FF, D_MODEL)) * D_FF ** -0.5,
        "even_w_in": nrm(ks[8], (N_EVEN, D_MODEL, EVEN_IN_WIDTH)) * D_MODEL ** -0.5,
        "even_conv_w": nrm(ks[9], (N_EVEN, CONV_WIDTH, 2 * MLSTM_WIDTH)) * CONV_WIDTH ** -0.5,
        "even_gate_b": gate_b,
        "even_head_norm_g": 1.0 + 0.02 * nrm(ks[12], (N_EVEN, N_MLSTM_HEADS + N_RET_HEADS, MLSTM_HEAD_DIM)),
        "even_w_out": nrm(ks[13], (N_EVEN, MIX_WIDTH, D_MODEL)) * MIX_WIDTH ** -0.5,
        "odd_w_in": nrm(ks[14], (N_ODD, D_MODEL, ODD_IN_WIDTH)) * D_MODEL ** -0.5,
        "odd_q_norm_g": 1.0 + 0.02 * nrm(ks[15], (N_ODD, ATTN_HEAD_DIM)),
        "odd_k_norm_g": 1.0 + 0.02 * nrm(ks[16], (N_ODD, ATTN_HEAD_DIM)),
        "odd_w_out": nrm(ks[17], (N_ODD, MIX_WIDTH, D_MODEL)) * MIX_WIDTH ** -0.5,
        "rel_bias": 0.5 * nrm(ks[18], (REL_BUCKETS, N_ATTN_HEADS)),
    }


def reference(x, c, ada_w, ada_b, norm1_g, norm2_g, mlp_w1, mlp_w2,
              even_w_in, even_conv_w, even_gate_b, even_head_norm_g, even_w_out,
              odd_w_in, odd_q_norm_g, odd_k_norm_g, odd_w_out, rel_bias):
    cond = jax.nn.silu(c)
    for l in range(DEPTH):
        mod = cond @ ada_w[l] + ada_b[l]
        sh1, sc1, g1, sh2, sc2, g2 = [m[:, None, :] for m in jnp.split(mod, 6, axis=-1)]
        h = rms_norm(x, norm1_g[l]) * (1.0 + sc1) + sh1
        if l % 2 == 0:
            e = l // 2
            y = mlstm_retention_mixer(h, even_w_in[e], even_conv_w[e], even_gate_b[e],
                                      even_head_norm_g[e], even_w_out[e])
        else:
            o = l // 2
            y = dsa_mixer(h, odd_w_in[o], odd_q_norm_g[o], odd_k_norm_g[o], rel_bias, odd_w_out[o])
        x = x + g1 * y
        h = rms_norm(x, norm2_g[l]) * (1.0 + sc2) + sh2
        x = x + g2 * sqrelu_mlp(h, mlp_w1[l], mlp_w2[l])
    return x
```

```python
import functools
import math

import numpy as np
import jax
import jax.numpy as jnp
from jax import lax
from jax.experimental import pallas as pl
from jax.experimental.pallas import tpu as pltpu

F32 = jnp.float32
BF16 = jnp.bfloat16
I32 = jnp.int32

D_MODEL = 2048
D_FF = 4 * D_MODEL
EPS = 1e-6
CHUNK = 128
HEAD_DIM = 256
N_GROUP_HEADS = 4
GROUP_WIDTH = N_GROUP_HEADS * HEAD_DIM
CONV_WIDTH = 4
GATE_SOFTCAP = 15.0
ROPE_BASE = 10000.0
N_ATTN_HEADS = 16
ATTN_DIM = 128
N_IDX_HEADS = 16
IDX_DIM = 64
MAX_TOPK = 256
Q_BLOCK = 128
REL_BUCKETS = 32
REL_MAX_DISTANCE = 128
LANES = 128
INT_MIN = -(2 ** 31)
NEG_BIG = -1e30
VMEM_LIMIT = 56 * 1024 * 1024


def _t5_large_bucket_thresholds():
    max_exact = REL_BUCKETS // 2
    d = np.arange(0, 4 * REL_MAX_DISTANCE)
    large = max_exact + (np.log(np.maximum(d, 1) / max_exact) / math.log(REL_MAX_DISTANCE / max_exact)
                         * (REL_BUCKETS - max_exact)).astype(np.int64)
    bucket = np.where(d < max_exact, d, np.minimum(large, REL_BUCKETS - 1))
    return [int(d[bucket >= b].min()) for b in range(max_exact + 1, REL_BUCKETS)]


T5_THRESHOLDS = _t5_large_bucket_thresholds()
assert T5_THRESHOLDS[-1] <= Q_BLOCK + 1


def _params(*semantics):
    return pltpu.CompilerParams(dimension_semantics=semantics, vmem_limit_bytes=VMEM_LIMIT)


def _dot(a, b):
    return jnp.dot(a, b, preferred_element_type=F32)


def _dot_nt(a, b):
    return lax.dot_general(a, b, (((1,), (1,)), ((), ())), preferred_element_type=F32)


def _dot_tn(a, b):
    return lax.dot_general(a, b, (((0,), (0,)), ((), ())), preferred_element_type=F32)


def _sigmoid(x):
    return 1.0 / (1.0 + jnp.exp(-x))


def _log_sigmoid(x):
    return jnp.minimum(x, 0.0) - jnp.log1p(jnp.exp(-jnp.abs(x)))


def _softcap(x):
    return GATE_SOFTCAP * jnp.tanh(x / GATE_SOFTCAP)


def _norm_modulate(x, g, sc, sh):
    r = lax.rsqrt(jnp.mean(x * x, axis=-1, keepdims=True) + EPS)
    return (x * r * g) * (1.0 + sc) + sh


def _ada_kernel(c_ref, w_ref, b_ref, o_ref):
    c = c_ref[...]
    cond = c * _sigmoid(c)
    o_ref[...] = jnp.dot(cond, w_ref[...], preferred_element_type=F32,
                         precision=lax.Precision.HIGHEST) + b_ref[...]


def _ada_modulation(c, ada_w, ada_b):
    depth, d, n = ada_w.shape
    bsz = c.shape[0]
    tn = 1024
    return pl.pallas_call(
        _ada_kernel,
        grid=(depth, n // tn),
        in_specs=[pl.BlockSpec((bsz, d), lambda l, j: (0, 0)),
                  pl.BlockSpec((None, d, tn), lambda l, j: (l, 0, j)),
                  pl.BlockSpec((None, 1, tn), lambda l, j: (l, 0, j))],
        out_specs=pl.BlockSpec((None, bsz, tn), lambda l, j: (l, 0, j)),
        out_shape=jax.ShapeDtypeStruct((depth, bsz, n), F32),
        compiler_params=_params("arbitrary", "arbitrary"),
        name="ada_modulation",
    )(c, ada_w, ada_b.reshape(depth, 1, n))


def _mod_spec(layer, which, bsz, tiles_per_batch):
    return pl.BlockSpec((None, 1, D_MODEL),
                        lambda i, *_: ((layer * bsz + i // tiles_per_batch) * 6 + which, 0, 0))


def _proj_kernel(x_ref, g_ref, sc_ref, sh_ref, w_ref, wt_ref, p_ref, t_ref, h_s):
    @pl.when(pl.program_id(1) == 0)
    def _():
        h = _norm_modulate(x_ref[...], g_ref[...], sc_ref[...], sh_ref[...]).astype(BF16)
        h_s[...] = h
        t_ref[...] = _dot(h, wt_ref[...])

    p_ref[...] = _dot(h_s[...], w_ref[...]).astype(p_ref.dtype)


def _in_projection(x2d, norm_g, mod3, layer, bsz, w_main, w_tail, tm, tn):
    m, d = x2d.shape
    n = w_main.shape[1]
    tpb = (m // bsz) // tm
    return pl.pallas_call(
        _proj_kernel,
        grid=(m // tm, n // tn),
        in_specs=[pl.BlockSpec((tm, d), lambda i, j: (i, 0)),
                  pl.BlockSpec((1, d), lambda i, j: (0, 0)),
                  _mod_spec(layer, 1, bsz, tpb),
                  _mod_spec(layer, 0, bsz, tpb),
                  pl.BlockSpec((d, tn), lambda i, j: (0, j)),
                  pl.BlockSpec((d, LANES), lambda i, j: (0, 0))],
        out_specs=[pl.BlockSpec((tm, tn), lambda i, j: (i, j)),
                   pl.BlockSpec((tm, LANES), lambda i, j: (i, 0))],
        out_shape=[jax.ShapeDtypeStruct((m, n), BF16),
                   jax.ShapeDtypeStruct((m, LANES), F32)],
        scratch_shapes=[pltpu.VMEM((tm, d), BF16)],
        compiler_params=_params("arbitrary", "arbitrary"),
        name=f"in_projection_{layer}",
    )(x2d, norm_g.reshape(1, d), mod3, mod3, w_main, w_tail)


def _rope_kernel(cos_ref, sin_ref):
    s, half = cos_ref.shape
    pos = lax.broadcasted_iota(I32, (s, half), 0).astype(F32)
    idx = lax.broadcasted_iota(I32, (s, half), 1).astype(F32)
    inv = jnp.exp(idx * (-math.log(ROPE_BASE) / half))
    ang = pos * inv
    cos_ref[...] = jnp.cos(ang)
    sin_ref[...] = jnp.sin(ang)


def _rope_tables(s):
    half = HEAD_DIM // 2
    return pl.pallas_call(
        _rope_kernel,
        out_shape=[jax.ShapeDtypeStruct((s, half), F32)] * 2,
        name="rope_tables",
    )()


def _group_norm_gate(h, ng, gate):
    r = lax.rsqrt(jnp.mean(h * h, axis=-1, keepdims=True) + EPS)
    return (h * r * ng * gate).astype(BF16)


def _mlstm_kernel(gb_ref, q_ref, k_ref, v_ref, o_ref, cwq_ref, cwk_ref, gc_ref, gbrow_ref, gr_ref,
                  ng_ref, y_ref, q_s, k_s, c_s, n_s, m_s):
    h = pl.program_id(1)
    s = q_ref.shape[0]
    rows = lax.broadcasted_iota(I32, (s, 1), 0)

    def conv_silu(x_ref, cw_ref):
        x = x_ref[...].astype(F32)
        acc = x * cw_ref[CONV_WIDTH - 1:CONV_WIDTH, :]
        for sft in range(1, CONV_WIDTH):
            xs = jnp.where(rows >= sft, pltpu.roll(x, sft, 0), 0.0)
            acc = acc + xs * cw_ref[CONV_WIDTH - 1 - sft:CONV_WIDTH - sft, :]
        return acc * _sigmoid(acc)

    q_s[...] = conv_silu(q_ref, cwq_ref).astype(BF16)
    k_s[...] = (conv_silu(k_ref, cwk_ref) * (HEAD_DIM ** -0.5)).astype(BF16)
    c_s[...] = jnp.zeros_like(c_s)
    n_s[...] = jnp.zeros_like(n_s)
    m_s[...] = jnp.zeros_like(m_s)

    lane = lax.broadcasted_iota(I32, (CHUNK, LANES), 1)
    ri = lax.broadcasted_iota(I32, (CHUNK, CHUNK), 0)
    ci = lax.broadcasted_iota(I32, (CHUNK, CHUNK), 1)
    tril = ri >= ci
    b_i = gb_ref[h]
    b_f = gb_ref[h + N_GROUP_HEADS]

    def body(c, carry):
        st = pl.multiple_of(c * CHUNK, CHUNK)
        gcap = _softcap(gc_ref[pl.ds(st, CHUNK), :] + gbrow_ref[...])
        li_col = jnp.sum(jnp.where(lane == h, gcap, 0.0), axis=1, keepdims=True)
        lf_col = jnp.sum(jnp.where(lane == h + N_GROUP_HEADS, _log_sigmoid(gcap), 0.0), axis=1, keepdims=True)
        li_row = _softcap(gr_ref[c, pl.ds(h, 1), :] + b_i)
        lf_row = _log_sigmoid(_softcap(gr_ref[c, pl.ds(h + N_GROUP_HEADS, 1), :] + b_f))
        bcum_col = jnp.sum(jnp.where(tril, lf_row, 0.0), axis=1, keepdims=True)
        bcum_row = jnp.sum(jnp.where(ri <= ci, lf_col, 0.0), axis=0, keepdims=True)
        b_last = jnp.sum(lf_row, axis=1, keepdims=True)
        m_prev = m_s[...]

        dmat = jnp.where(tril, bcum_col - bcum_row + li_row, -jnp.inf)
        m_inter = bcum_col + m_prev
        m_row = jnp.maximum(jnp.max(dmat, axis=1, keepdims=True), m_inter)
        qc = q_s[pl.ds(st, CHUNK), :]
        kc = k_s[pl.ds(st, CHUNK), :]
        vc = v_ref[pl.ds(st, CHUNK), :]
        sc = _dot_nt(qc, kc) * jnp.exp(dmat - m_row)
        inter = jnp.exp(m_inter - m_row)
        num = _dot(sc.astype(BF16), vc) + inter * _dot(qc, c_s[...].astype(BF16))
        den = (jnp.sum(sc, axis=1, keepdims=True)
               + inter * jnp.sum(qc.astype(F32) * n_s[...], axis=1, keepdims=True))
        h_out = num / jnp.maximum(jnp.abs(den), jnp.exp(-m_row))

        g_col = b_last - bcum_col + li_col
        m_new = jnp.maximum(b_last + m_prev, jnp.max(g_col, axis=0, keepdims=True))
        decay = jnp.exp(b_last + m_prev - m_new)
        kw = kc.astype(F32) * jnp.exp(g_col - m_new)
        c_s[...] = decay * c_s[...] + _dot_tn(kw.astype(BF16), vc)
        n_s[...] = decay * n_s[...] + jnp.sum(kw, axis=0, keepdims=True)
        m_s[...] = m_new

        gate = _sigmoid(o_ref[pl.ds(st, CHUNK), :].astype(F32))
        y_ref[pl.ds(st, CHUNK), :] = _group_norm_gate(h_out, ng_ref[...], gate)
        return carry

    lax.fori_loop(0, s // CHUNK, body, 0)


def _mlstm_heads(p3, gates_col, gates_row, conv_w, gate_b, head_norm_g):
    bsz, s, _ = p3.shape
    nh = N_GROUP_HEADS
    hd = HEAD_DIM

    def col(off):
        return pl.BlockSpec((None, s, hd), lambda b, h: (b, 0, off + h))

    gb_row = jnp.zeros((1, LANES), F32).at[0, :2 * nh].set(gate_b)
    return pl.pallas_call(
        _mlstm_kernel,
        grid=(bsz, nh),
        in_specs=[pl.BlockSpec(memory_space=pltpu.SMEM),
                  col(0), col(nh), col(2 * nh), col(3 * nh),
                  pl.BlockSpec((CONV_WIDTH, hd), lambda b, h: (0, h)),
                  pl.BlockSpec((CONV_WIDTH, hd), lambda b, h: (0, nh + h)),
                  pl.BlockSpec((None, s, LANES), lambda b, h: (b, 0, 0)),
                  pl.BlockSpec((1, LANES), lambda b, h: (0, 0)),
                  pl.BlockSpec((None, s // CHUNK, 2 * nh, CHUNK), lambda b, h: (b, 0, 0, 0)),
                  pl.BlockSpec((None, 1, hd), lambda b, h: (h, 0, 0))],
        out_specs=pl.BlockSpec((None, s, hd), lambda b, h: (b, 0, h)),
        out_shape=jax.ShapeDtypeStruct((bsz, s, GROUP_WIDTH), BF16),
        scratch_shapes=[pltpu.VMEM((s, hd), BF16), pltpu.VMEM((s, hd), BF16),
                        pltpu.VMEM((hd, hd), F32), pltpu.VMEM((1, hd), F32), pltpu.VMEM((1, 1), F32)],
        compiler_params=_params("arbitrary", "arbitrary"),
        name="mlstm_heads",
    )(gate_b, p3, p3, p3, p3, conv_w, conv_w, gates_col, gb_row, gates_row,
      head_norm_g.reshape(2 * nh, 1, hd))


def _ret_kernel(q_ref, k_ref, v_ref, g_ref, cos_ref, sin_ref, ng_ref, y_ref, q_s, k_s, r_s):
    h = pl.program_id(1)
    s = q_ref.shape[0]
    half = HEAD_DIM // 2
    cos = cos_ref[...]
    sin = sin_ref[...]

    def rotate(x_ref, dst, scale):
        x1 = x_ref[:, :half].astype(F32)
        x2 = x_ref[:, half:].astype(F32)
        dst[:, :half] = ((x1 * cos - x2 * sin) * scale).astype(BF16)
        dst[:, half:] = ((x2 * cos + x1 * sin) * scale).astype(BF16)

    rotate(q_ref, q_s, 1.0)
    rotate(k_ref, k_s, HEAD_DIM ** -0.5)
    r_s[...] = jnp.zeros_like(r_s)

    log_gamma = jnp.full((1, 1), math.log(1.0 - 2.0 ** -5.0), F32)
    for i in range(1, N_GROUP_HEADS):
        log_gamma = jnp.where(h == i, math.log(1.0 - 2.0 ** (-5.0 - i)), log_gamma)
    ri = lax.broadcasted_iota(I32, (CHUNK, CHUNK), 0)
    ci = lax.broadcasted_iota(I32, (CHUNK, CHUNK), 1)
    rel = (ri - ci).astype(F32)
    dmask = jnp.where(rel >= 0, jnp.exp(jnp.maximum(rel, 0.0) * log_gamma), 0.0)
    jcol = lax.broadcasted_iota(I32, (CHUNK, 1), 0).astype(F32)
    cross_decay = jnp.exp((jcol + 1.0) * log_gamma)
    state_decay = jnp.exp((CHUNK - 1.0 - jcol) * log_gamma)
    chunk_decay = jnp.exp(CHUNK * log_gamma)

    def body(c, carry):
        st = pl.multiple_of(c * CHUNK, CHUNK)
        qc = q_s[pl.ds(st, CHUNK), :]
        kc = k_s[pl.ds(st, CHUNK), :]
        vc = v_ref[pl.ds(st, CHUNK), :]
        inner = _dot((_dot_nt(qc, kc) * dmask).astype(BF16), vc)
        cross = _dot(qc, r_s[...].astype(BF16)) * cross_decay
        r_s[...] = chunk_decay * r_s[...] + _dot_tn((kc.astype(F32) * state_decay).astype(BF16), vc)
        g = g_ref[pl.ds(st, CHUNK), :].astype(F32)
        y_ref[pl.ds(st, CHUNK), :] = _group_norm_gate(inner + cross, ng_ref[...], g * _sigmoid(g))
        return carry

    lax.fori_loop(0, s // CHUNK, body, 0)


def _retention_heads(p3, cos, sin, head_norm_g):
    bsz, s, _ = p3.shape
    nh = N_GROUP_HEADS
    hd = HEAD_DIM

    def col(off):
        return pl.BlockSpec((None, s, hd), lambda b, h: (b, 0, off + h))

    return pl.pallas_call(
        _ret_kernel,
        grid=(bsz, nh),
        in_specs=[col(4 * nh), col(5 * nh), col(6 * nh), col(7 * nh),
                  pl.BlockSpec((s, hd // 2), lambda b, h: (0, 0)),
                  pl.BlockSpec((s, hd // 2), lambda b, h: (0, 0)),
                  pl.BlockSpec((None, 1, hd), lambda b, h: (nh + h, 0, 0))],
        out_specs=pl.BlockSpec((None, s, hd), lambda b, h: (b, 0, h)),
        out_shape=jax.ShapeDtypeStruct((bsz, s, GROUP_WIDTH), BF16),
        scratch_shapes=[pltpu.VMEM((s, hd), BF16), pltpu.VMEM((s, hd), BF16), pltpu.VMEM((hd, hd), F32)],
        compiler_params=_params("arbitrary", "arbitrary"),
        name="retention_heads",
    )(p3, p3, p3, p3, cos, sin, head_norm_g.reshape(2 * nh, 1, hd))


def _dsa_kernel(tab_ref, q_ref, iq_ref, iw_ref, k_ref, v_ref, ika_ref, ikb_ref, qg_ref, kg_ref, o_ref,
                kn_s, ikab_s, vaug_s, key_s, tb_s, wb_s, qall_s, iqall_s, acc_s, m_s, *, topk):
    b = pl.program_id(0)
    qb = pl.program_id(1)
    s = k_ref.shape[0]
    nkb = s // Q_BLOCK
    ri = lax.broadcasted_iota(I32, (Q_BLOCK, Q_BLOCK), 0)
    ci = lax.broadcasted_iota(I32, (Q_BLOCK, Q_BLOCK), 1)

    @pl.when((b == 0) & (qb == 0))
    def _():
        for back in range(3):
            dist = jnp.maximum(back * Q_BLOCK + ri - ci, 0)
            bucket = jnp.full(dist.shape, REL_BUCKETS // 2, I32)
            for thr in T5_THRESHOLDS:
                bucket = bucket + jnp.where(dist >= thr, 1, 0)
            bucket = jnp.where(dist < REL_BUCKETS // 2, dist, bucket)

            def head_body(hh, carry, back=back, bucket=bucket):
                tile = jnp.zeros(bucket.shape, F32)
                for bk in range(REL_BUCKETS):
                    tile = jnp.where(bucket == bk, tab_ref[bk, hh], tile)
                tb_s[back, hh] = tile
                return carry

            lax.fori_loop(0, N_ATTN_HEADS, head_body, 0)

    @pl.when(qb == 0)
    def _():
        k = k_ref[...].astype(F32)
        r = lax.rsqrt(jnp.mean(k * k, axis=-1, keepdims=True) + EPS)
        kn_s[...] = (k * r * kg_ref[...]).astype(BF16)
        for kt in range(nkb):
            ikab_s[kt, :Q_BLOCK, :] = ika_ref[kt * Q_BLOCK:(kt + 1) * Q_BLOCK, :]
            ikab_s[kt, Q_BLOCK:, :] = ikb_ref[kt * Q_BLOCK:(kt + 1) * Q_BLOCK, :]
        vaug_s[:, :ATTN_DIM] = v_ref[...]
        lane = lax.broadcasted_iota(I32, (s, ATTN_DIM), 1)
        vaug_s[:, ATTN_DIM:] = jnp.where(lane == 0, 1.0, 0.0).astype(BF16)

    for hh in range(N_ATTN_HEADS):
        qh = q_ref[:, hh * ATTN_DIM:(hh + 1) * ATTN_DIM].astype(F32)
        r = lax.rsqrt(jnp.mean(qh * qh, axis=-1, keepdims=True) + EPS)
        qall_s[hh * Q_BLOCK:(hh + 1) * Q_BLOCK, :] = (qh * r * qg_ref[...] * (ATTN_DIM ** -0.5)).astype(BF16)
    for hp in range(N_IDX_HEADS // 2):
        iqall_s[hp * Q_BLOCK:(hp + 1) * Q_BLOCK, :] = iq_ref[:, hp * LANES:(hp + 1) * LANES]
    wv = iw_ref[...] * ((N_IDX_HEADS ** -0.5) * (IDX_DIM ** -0.5))
    for hh in range(N_IDX_HEADS):
        wb_s[hh] = jnp.broadcast_to(wv[:, hh:hh + 1], (Q_BLOCK, Q_BLOCK))

    def score_body(kt, carry):
        r = _dot_nt(iqall_s[...], ikab_s[kt])
        acc = jnp.zeros((Q_BLOCK, Q_BLOCK), F32)
        for hp in range(N_IDX_HEADS // 2):
            blk = r[hp * Q_BLOCK:(hp + 1) * Q_BLOCK]
            acc = acc + jnp.maximum(blk[:, :Q_BLOCK], 0.0) * wb_s[2 * hp]
            acc = acc + jnp.maximum(blk[:, Q_BLOCK:], 0.0) * wb_s[2 * hp + 1]
        bits = pltpu.bitcast(acc, I32)
        key = bits ^ ((bits >> 31) & 0x7FFFFFFF)
        causal = (kt * Q_BLOCK + ci) <= (qb * Q_BLOCK + ri)
        key_s[kt] = jnp.where(causal, key, INT_MIN)
        return carry

    lax.fori_loop(0, qb + 1, score_body, 0)

    def count_ge(cand):
        def inner(kt, acc):
            return acc + jnp.where(key_s[kt] >= cand, 1.0, 0.0)
        acc = lax.fori_loop(0, qb + 1, inner, jnp.zeros((Q_BLOCK, Q_BLOCK), F32))
        return jnp.sum(acc, axis=1, keepdims=True)

    zero = jnp.zeros((Q_BLOCK, 1), I32)
    thr0 = jnp.where(count_ge(zero) >= topk, zero, INT_MIN)

    def bit_body(i, thr):
        cand = thr + jnp.left_shift(jnp.int32(1), 30 - i)
        return jnp.where(count_ge(cand) >= topk, cand, thr)

    thr = lax.fori_loop(0, 31, bit_body, thr0)
    thr = jnp.maximum(thr, INT_MIN + 1)

    acc_s[...] = jnp.zeros_like(acc_s)
    m_s[...] = jnp.full(m_s.shape, NEG_BIG, F32)

    def attn_body(kt, carry):
        st = pl.multiple_of(kt * Q_BLOCK, Q_BLOCK)
        mask_bias = jnp.where(key_s[kt] >= thr, 0.0, NEG_BIG)
        back = jnp.minimum(qb - kt, 2)
        sc = _dot_nt(qall_s[...], kn_s[pl.ds(st, Q_BLOCK), :]).reshape(N_ATTN_HEADS, Q_BLOCK, Q_BLOCK)
        sc = sc + tb_s[back] + mask_bias[None]
        m_prev = m_s[...]
        m_new = jnp.maximum(m_prev, jnp.max(sc, axis=-1, keepdims=True))
        alpha = jnp.exp(m_prev - m_new)
        p = jnp.exp(sc - m_new).reshape(N_ATTN_HEADS * Q_BLOCK, Q_BLOCK)
        pv = _dot(p.astype(BF16), vaug_s[pl.ds(st, Q_BLOCK), :])
        acc_s[...] = alpha.reshape(N_ATTN_HEADS * Q_BLOCK, 1) * acc_s[...] + pv
        m_s[...] = m_new
        return carry

    lax.fori_loop(0, qb + 1, attn_body, 0)

    out = acc_s[:, :ATTN_DIM] / acc_s[:, ATTN_DIM:ATTN_DIM + 1]
    for hh in range(N_ATTN_HEADS):
        o_ref[:, hh * ATTN_DIM:(hh + 1) * ATTN_DIM] = out[hh * Q_BLOCK:(hh + 1) * Q_BLOCK].astype(BF16)


def _dsa_attention(p3, iw3, q_norm_g, k_norm_g, rel_bias):
    bsz, s, _ = p3.shape
    nq = s // Q_BLOCK
    qw = N_ATTN_HEADS * ATTN_DIM
    iqw = N_IDX_HEADS * IDX_DIM
    small0 = (qw + iqw) // LANES

    def small(off):
        return pl.BlockSpec((None, s, LANES), lambda b, i: (b, 0, small0 + off))

    return pl.pallas_call(
        functools.partial(_dsa_kernel, topk=min(MAX_TOPK, s // 4)),
        grid=(bsz, nq),
        in_specs=[pl.BlockSpec(memory_space=pltpu.SMEM),
                  pl.BlockSpec((None, Q_BLOCK, qw), lambda b, i: (b, i, 0)),
                  pl.BlockSpec((None, Q_BLOCK, iqw), lambda b, i: (b, i, qw // iqw)),
                  pl.BlockSpec((None, Q_BLOCK, LANES), lambda b, i: (b, i, 0)),
                  small(0), small(1), small(2), small(3),
                  pl.BlockSpec((1, ATTN_DIM), lambda b, i: (0, 0)),
                  pl.BlockSpec((1, ATTN_DIM), lambda b, i: (0, 0))],
        out_specs=pl.BlockSpec((None, Q_BLOCK, qw), lambda b, i: (b, i, 0)),
        out_shape=jax.ShapeDtypeStruct((bsz, s, qw), BF16),
        scratch_shapes=[pltpu.VMEM((s, ATTN_DIM), BF16),
                        pltpu.VMEM((nq, 2 * Q_BLOCK, LANES), BF16),
                        pltpu.VMEM((s, 2 * ATTN_DIM), BF16),
                        pltpu.VMEM((nq, Q_BLOCK, Q_BLOCK), I32),
                        pltpu.VMEM((3, N_ATTN_HEADS, Q_BLOCK, Q_BLOCK), F32),
                        pltpu.VMEM((N_IDX_HEADS, Q_BLOCK, Q_BLOCK), F32),
                        pltpu.VMEM((N_ATTN_HEADS * Q_BLOCK, ATTN_DIM), BF16),
                        pltpu.VMEM((N_IDX_HEADS // 2 * Q_BLOCK, LANES), BF16),
                        pltpu.VMEM((N_ATTN_HEADS * Q_BLOCK, 2 * ATTN_DIM), F32),
                        pltpu.VMEM((N_ATTN_HEADS, Q_BLOCK, 1), F32)],
        compiler_params=_params("arbitrary", "arbitrary"),
        name="dsa_attention",
    )(rel_bias, p3, p3, iw3, p3, p3, p3, p3, q_norm_g.reshape(1, ATTN_DIM), k_norm_g.reshape(1, ATTN_DIM))


def _outproj_kernel(ya_ref, yb_ref, x_ref, w_ref, g1_ref, n2_ref, sc2_ref, sh2_ref, x1_ref, h2_ref):
    half = ya_ref.shape[1]
    mix = _dot(ya_ref[...], w_ref[:half, :]) + _dot(yb_ref[...], w_ref[half:, :])
    x1 = x_ref[...] + g1_ref[...] * mix
    x1_ref[...] = x1
    h2_ref[...] = _norm_modulate(x1, n2_ref[...], sc2_ref[...], sh2_ref[...]).astype(BF16)


def _out_projection(ya, yb, yb_col, x2d, w_out, norm2_g, mod3, layer, bsz, tm):
    m, d = x2d.shape
    tpb = (m // bsz) // tm
    row = pl.BlockSpec((tm, d), lambda i: (i, 0))
    return pl.pallas_call(
        _outproj_kernel,
        grid=(m // tm,),
        in_specs=[pl.BlockSpec((tm, d // 2), lambda i: (i, 0)),
                  pl.BlockSpec((tm, d // 2), lambda i: (i, yb_col)),
                  row,
                  pl.BlockSpec((d, d), lambda i: (0, 0)),
                  _mod_spec(layer, 2, bsz, tpb),
                  pl.BlockSpec((1, d), lambda i: (0, 0)),
                  _mod_spec(layer, 4, bsz, tpb),
                  _mod_spec(layer, 3, bsz, tpb)],
        out_specs=[row, row],
        out_shape=[jax.ShapeDtypeStruct((m, d), F32), jax.ShapeDtypeStruct((m, d), BF16)],
        compiler_params=_params("arbitrary"),
        name=f"out_projection_{layer}",
    )(ya, yb, x2d, w_out, mod3, norm2_g.reshape(1, d), mod3, mod3)


def _mlp_kernel(h_ref, x1_ref, w1_ref, w2_ref, g2_ref, o_ref):
    j = pl.program_id(1)
    a = jnp.square(jnp.maximum(_dot(h_ref[...], w1_ref[...]), 0.0)).astype(BF16)
    part = _dot(a, w2_ref[...])

    @pl.when(j == 0)
    def _():
        o_ref[...] = part

    @pl.when(j > 0)
    def _():
        o_ref[...] += part

    @pl.when(j == pl.num_programs(1) - 1)
    def _():
        o_ref[...] = x1_ref[...] + g2_ref[...] * o_ref[...]


def _mlp(h2, x1, w1, w2, mod3, layer, bsz, tm, tf):
    m, d = x1.shape
    ff = w1.shape[1]
    tpb = (m // bsz) // tm
    row = pl.BlockSpec((tm, d), lambda i, j: (i, 0))
    return pl.pallas_call(
        _mlp_kernel,
        grid=(m // tm, ff // tf),
        in_specs=[row, row,
                  pl.BlockSpec((d, tf), lambda i, j: (0, j)),
                  pl.BlockSpec((tf, d), lambda i, j: (j, 0)),
                  _mod_spec(layer, 5, bsz, tpb)],
        out_specs=row,
        out_shape=jax.ShapeDtypeStruct((m, d), F32),
        compiler_params=_params("arbitrary", "arbitrary"),
        name=f"mlp_{layer}",
    )(h2, x1, w1, w2, mod3)


def _pad_cols(w, width):
    return jnp.pad(w, ((0, 0), (0, width - w.shape[1])))


def _even_weights(w_in):
    gw = GROUP_WIDTH
    g0 = 4 * gw
    g1 = g0 + 2 * N_GROUP_HEADS
    main = jnp.concatenate([w_in[:, :g0], w_in[:, g1:]], axis=1).astype(BF16)
    tail = _pad_cols(w_in[:, g0:g1], LANES).astype(BF16)
    return main, tail


def _odd_weights(w_in):
    qw = N_ATTN_HEADS * ATTN_DIM
    iqw = N_IDX_HEADS * IDX_DIM
    o_k, o_v, o_iq = qw, qw + ATTN_DIM, qw + 2 * ATTN_DIM
    o_ik = o_iq + iqw
    o_iw = o_ik + IDX_DIM
    w_ik = w_in[:, o_ik:o_iw]
    zeros = jnp.zeros_like(w_ik)
    main = jnp.concatenate([w_in[:, :qw], w_in[:, o_iq:o_ik], w_in[:, o_k:o_v], w_in[:, o_v:o_iq],
                            w_ik, zeros, zeros, w_ik], axis=1).astype(BF16)
    tail = _pad_cols(w_in[:, o_iw:], LANES).astype(BF16)
    return main, tail


def kernel(x, c, ada_w, ada_b, norm1_g, norm2_g, mlp_w1, mlp_w2, even_w_in, even_conv_w, even_gate_b,
           even_head_norm_g, even_w_out, odd_w_in, odd_q_norm_g, odd_k_norm_g, odd_w_out, rel_bias):
    bsz, s, d = x.shape
    depth = ada_w.shape[0]
    m = bsz * s
    tm = 512
    mod3 = _ada_modulation(c, ada_w, ada_b).reshape(depth * bsz * 6, 1, d)
    cos, sin = _rope_tables(s)
    xc = x.reshape(m, d)
    for l in range(depth):
        e = l // 2
        if l % 2 == 0:
            w_main, w_tail = _even_weights(even_w_in[e])
            p, gates = _in_projection(xc, norm1_g[l], mod3, l, bsz, w_main, w_tail, tm, 1024)
            p3 = p.reshape(bsz, s, -1)
            gates_col = gates.reshape(bsz, s, LANES)
            gates_row = jnp.swapaxes(
                gates_col[:, :, :2 * N_GROUP_HEADS].reshape(bsz, s // CHUNK, CHUNK, 2 * N_GROUP_HEADS), 2, 3)
            ya = _mlstm_heads(p3, gates_col, gates_row, even_conv_w[e], even_gate_b[e], even_head_norm_g[e])
            yb = _retention_heads(p3, cos, sin, even_head_norm_g[e])
            ya, yb, yb_col = ya.reshape(m, d // 2), yb.reshape(m, d // 2), 0
            w_out = even_w_out[e]
        else:
            w_main, w_tail = _odd_weights(odd_w_in[e])
            p, iw = _in_projection(xc, norm1_g[l], mod3, l, bsz, w_main, w_tail, tm, 512)
            y = _dsa_attention(p.reshape(bsz, s, -1), iw.reshape(bsz, s, LANES),
                               odd_q_norm_g[e], odd_k_norm_g[e], rel_bias)
            ya = yb = y.reshape(m, d)
            yb_col = 1
            w_out = odd_w_out[e]
        x1, h2 = _out_projection(ya, yb, yb_col, xc, w_out.astype(BF16), norm2_g[l], mod3, l, bsz, tm)
        xc = _mlp(h2, x1, mlp_w1[l].astype(BF16), mlp_w2[l].astype(BF16), mod3, l, bsz, tm, 512)
    return xc.reshape(bsz, s, d)
```

```python
import functools
import math

import numpy as np
import jax
import jax.numpy as jnp
from jax import lax
from jax.experimental import pallas as pl
from jax.experimental.pallas import tpu as pltpu

F32 = jnp.float32
BF16 = jnp.bfloat16
I32 = jnp.int32

D_MODEL = 2048
D_FF = 4 * D_MODEL
EPS = 1e-6
CHUNK = 128
HEAD_DIM = 256
N_GROUP_HEADS = 4
GROUP_WIDTH = N_GROUP_HEADS * HEAD_DIM
CONV_WIDTH = 4
GATE_SOFTCAP = 15.0
ROPE_BASE = 10000.0
N_ATTN_HEADS = 16
ATTN_DIM = 128
N_IDX_HEADS = 16
IDX_DIM = 64
MAX_TOPK = 256
Q_BLOCK = 128
KEY_TILE = 2 * Q_BLOCK
REL_BUCKETS = 32
REL_MAX_DISTANCE = 128
LANES = 128
COUNT_ROWS = 64
INT_MIN = -(2 ** 31)
NEG_BIG = -1e30
VMEM_LIMIT = 56 * 1024 * 1024


def _t5_large_bucket_thresholds():
    max_exact = REL_BUCKETS // 2
    d = np.arange(0, 4 * REL_MAX_DISTANCE)
    large = max_exact + (np.log(np.maximum(d, 1) / max_exact) / math.log(REL_MAX_DISTANCE / max_exact)
                         * (REL_BUCKETS - max_exact)).astype(np.int64)
    bucket = np.where(d < max_exact, d, np.minimum(large, REL_BUCKETS - 1))
    return [int(d[bucket >= b].min()) for b in range(max_exact + 1, REL_BUCKETS)]


T5_THRESHOLDS = _t5_large_bucket_thresholds()
assert T5_THRESHOLDS[-1] <= Q_BLOCK + 1


def _params(*semantics):
    return pltpu.CompilerParams(dimension_semantics=semantics, vmem_limit_bytes=VMEM_LIMIT)


def _dot(a, b):
    return jnp.dot(a, b, preferred_element_type=F32)


def _dot_nt(a, b):
    return lax.dot_general(a, b, (((1,), (1,)), ((), ())), preferred_element_type=F32)


def _dot_tn(a, b):
    return lax.dot_general(a, b, (((0,), (0,)), ((), ())), preferred_element_type=F32)


def _sigmoid(x):
    return 1.0 / (1.0 + jnp.exp(-x))


def _log_sigmoid(x):
    return jnp.minimum(x, 0.0) - jnp.log1p(jnp.exp(-jnp.abs(x)))


def _softcap(x):
    return GATE_SOFTCAP * jnp.tanh(x / GATE_SOFTCAP)


def _norm_modulate(x, g, sc, sh):
    r = lax.rsqrt(jnp.mean(x * x, axis=-1, keepdims=True) + EPS)
    return (x * r * g) * (1.0 + sc) + sh


def _ada_kernel(c_ref, w_ref, b_ref, o_ref):
    c = c_ref[...]
    cond = c * _sigmoid(c)
    o_ref[...] = jnp.dot(cond, w_ref[...], preferred_element_type=F32,
                         precision=lax.Precision.HIGHEST) + b_ref[...]


def _ada_modulation(c, ada_w, ada_b):
    depth, d, n = ada_w.shape
    bsz = c.shape[0]
    tn = 1024
    return pl.pallas_call(
        _ada_kernel,
        grid=(depth, n // tn),
        in_specs=[pl.BlockSpec((bsz, d), lambda l, j: (0, 0)),
                  pl.BlockSpec((None, d, tn), lambda l, j: (l, 0, j)),
                  pl.BlockSpec((None, 1, tn), lambda l, j: (l, 0, j))],
        out_specs=pl.BlockSpec((None, bsz, tn), lambda l, j: (l, 0, j)),
        out_shape=jax.ShapeDtypeStruct((depth, bsz, n), F32),
        compiler_params=_params("arbitrary", "arbitrary"),
        name="ada_modulation",
    )(c, ada_w, ada_b.reshape(depth, 1, n))


def _mod_spec(layer, which, bsz, tiles_per_batch):
    return pl.BlockSpec((None, 1, D_MODEL),
                        lambda i, *_: ((layer * bsz + i // tiles_per_batch) * 6 + which, 0, 0))


def _proj_kernel(x_ref, g_ref, sc_ref, sh_ref, w_ref, wt_ref, p_ref, t_ref, h_s):
    @pl.when(pl.program_id(1) == 0)
    def _():
        h = _norm_modulate(x_ref[...], g_ref[...], sc_ref[...], sh_ref[...]).astype(BF16)
        h_s[...] = h
        t_ref[...] = _dot(h, wt_ref[...])

    p_ref[...] = _dot(h_s[...], w_ref[...]).astype(p_ref.dtype)


def _in_projection(x2d, norm_g, mod3, layer, bsz, w_main, w_tail, tm, tn):
    m, d = x2d.shape
    n = w_main.shape[1]
    tpb = (m // bsz) // tm
    return pl.pallas_call(
        _proj_kernel,
        grid=(m // tm, n // tn),
        in_specs=[pl.BlockSpec((tm, d), lambda i, j: (i, 0)),
                  pl.BlockSpec((1, d), lambda i, j: (0, 0)),
                  _mod_spec(layer, 1, bsz, tpb),
                  _mod_spec(layer, 0, bsz, tpb),
                  pl.BlockSpec((d, tn), lambda i, j: (0, j)),
                  pl.BlockSpec((d, LANES), lambda i, j: (0, 0))],
        out_specs=[pl.BlockSpec((tm, tn), lambda i, j: (i, j)),
                   pl.BlockSpec((tm, LANES), lambda i, j: (i, 0))],
        out_shape=[jax.ShapeDtypeStruct((m, n), BF16),
                   jax.ShapeDtypeStruct((m, LANES), F32)],
        scratch_shapes=[pltpu.VMEM((tm, d), BF16)],
        compiler_params=_params("arbitrary", "arbitrary"),
        name=f"in_projection_{layer}",
    )(x2d, norm_g.reshape(1, d), mod3, mod3, w_main, w_tail)


def _rope_kernel(cos_ref, sin_ref):
    s, half = cos_ref.shape
    pos = lax.broadcasted_iota(I32, (s, half), 0).astype(F32)
    idx = lax.broadcasted_iota(I32, (s, half), 1).astype(F32)
    inv = jnp.exp(idx * (-math.log(ROPE_BASE) / half))
    ang = pos * inv
    cos_ref[...] = jnp.cos(ang)
    sin_ref[...] = jnp.sin(ang)


def _rope_tables(s):
    half = HEAD_DIM // 2
    return pl.pallas_call(
        _rope_kernel,
        out_shape=[jax.ShapeDtypeStruct((s, half), F32)] * 2,
        name="rope_tables",
    )()


def _group_norm_gate(h, ng, gate):
    r = lax.rsqrt(jnp.mean(h * h, axis=-1, keepdims=True) + EPS)
    return (h * r * ng * gate).astype(BF16)


def _mlstm_kernel(gb_ref, q_ref, k_ref, v_ref, o_ref, cwq_ref, cwk_ref, gc_ref, gbrow_ref, gr_ref,
                  ng_ref, y_ref, q_s, k_s, c_s, n_s, m_s):
    h = pl.program_id(1)
    s = q_ref.shape[0]
    rows = lax.broadcasted_iota(I32, (s, 1), 0)

    def conv_silu(x_ref, cw_ref):
        x = x_ref[...].astype(F32)
        acc = x * cw_ref[CONV_WIDTH - 1:CONV_WIDTH, :]
        for sft in range(1, CONV_WIDTH):
            xs = jnp.where(rows >= sft, pltpu.roll(x, sft, 0), 0.0)
            acc = acc + xs * cw_ref[CONV_WIDTH - 1 - sft:CONV_WIDTH - sft, :]
        return acc * _sigmoid(acc)

    q_s[...] = conv_silu(q_ref, cwq_ref).astype(BF16)
    k_s[...] = (conv_silu(k_ref, cwk_ref) * (HEAD_DIM ** -0.5)).astype(BF16)
    c_s[...] = jnp.zeros_like(c_s)
    n_s[...] = jnp.zeros_like(n_s)
    m_s[...] = jnp.zeros_like(m_s)

    lane = lax.broadcasted_iota(I32, (CHUNK, LANES), 1)
    ri = lax.broadcasted_iota(I32, (CHUNK, CHUNK), 0)
    ci = lax.broadcasted_iota(I32, (CHUNK, CHUNK), 1)
    tril = ri >= ci
    b_i = gb_ref[h]
    b_f = gb_ref[h + N_GROUP_HEADS]

    def body(c, carry):
        st = pl.multiple_of(c * CHUNK, CHUNK)
        gcap = _softcap(gc_ref[pl.ds(st, CHUNK), :] + gbrow_ref[...])
        li_col = jnp.sum(jnp.where(lane == h, gcap, 0.0), axis=1, keepdims=True)
        lf_col = jnp.sum(jnp.where(lane == h + N_GROUP_HEADS, _log_sigmoid(gcap), 0.0), axis=1, keepdims=True)
        li_row = _softcap(gr_ref[c, pl.ds(h, 1), :] + b_i)
        lf_row = _log_sigmoid(_softcap(gr_ref[c, pl.ds(h + N_GROUP_HEADS, 1), :] + b_f))
        bcum_col = jnp.sum(jnp.where(tril, lf_row, 0.0), axis=1, keepdims=True)
        bcum_row = jnp.sum(jnp.where(ri <= ci, lf_col, 0.0), axis=0, keepdims=True)
        b_last = jnp.sum(lf_row, axis=1, keepdims=True)
        m_prev = m_s[...]

        dmat = jnp.where(tril, bcum_col - bcum_row + li_row, -jnp.inf)
        m_inter = bcum_col + m_prev
        m_row = jnp.maximum(jnp.max(dmat, axis=1, keepdims=True), m_inter)
        qc = q_s[pl.ds(st, CHUNK), :]
        kc = k_s[pl.ds(st, CHUNK), :]
        vc = v_ref[pl.ds(st, CHUNK), :]
        sc = _dot_nt(qc, kc) * jnp.exp(dmat - m_row)
        inter = jnp.exp(m_inter - m_row)
        num = _dot(sc.astype(BF16), vc) + inter * _dot(qc, c_s[...].astype(BF16))
        den = (jnp.sum(sc, axis=1, keepdims=True)
               + inter * jnp.sum(qc.astype(F32) * n_s[...], axis=1, keepdims=True))
        h_out = num / jnp.maximum(jnp.abs(den), jnp.exp(-m_row))

        g_col = b_last - bcum_col + li_col
        m_new = jnp.maximum(b_last + m_prev, jnp.max(g_col, axis=0, keepdims=True))
        decay = jnp.exp(b_last + m_prev - m_new)
        kw = kc.astype(F32) * jnp.exp(g_col - m_new)
        c_s[...] = decay * c_s[...] + _dot_tn(kw.astype(BF16), vc)
        n_s[...] = decay * n_s[...] + jnp.sum(kw, axis=0, keepdims=True)
        m_s[...] = m_new

        gate = _sigmoid(o_ref[pl.ds(st, CHUNK), :].astype(F32))
        y_ref[pl.ds(st, CHUNK), :] = _group_norm_gate(h_out, ng_ref[...], gate)
        return carry

    lax.fori_loop(0, s // CHUNK, body, 0)


def _mlstm_heads(p3, gates_col, gates_row, conv_w, gate_b, head_norm_g):
    bsz, s, _ = p3.shape
    nh = N_GROUP_HEADS
    hd = HEAD_DIM

    def col(off):
        return pl.BlockSpec((None, s, hd), lambda b, h: (b, 0, off + h))

    gb_row = jnp.zeros((1, LANES), F32).at[0, :2 * nh].set(gate_b)
    return pl.pallas_call(
        _mlstm_kernel,
        grid=(bsz, nh),
        in_specs=[pl.BlockSpec(memory_space=pltpu.SMEM),
                  col(0), col(nh), col(2 * nh), col(3 * nh),
                  pl.BlockSpec((CONV_WIDTH, hd), lambda b, h: (0, h)),
                  pl.BlockSpec((CONV_WIDTH, hd), lambda b, h: (0, nh + h)),
                  pl.BlockSpec((None, s, LANES), lambda b, h: (b, 0, 0)),
                  pl.BlockSpec((1, LANES), lambda b, h: (0, 0)),
                  pl.BlockSpec((None, s // CHUNK, 2 * nh, CHUNK), lambda b, h: (b, 0, 0, 0)),
                  pl.BlockSpec((None, 1, hd), lambda b, h: (h, 0, 0))],
        out_specs=pl.BlockSpec((None, s, hd), lambda b, h: (b, 0, h)),
        out_shape=jax.ShapeDtypeStruct((bsz, s, GROUP_WIDTH), BF16),
        scratch_shapes=[pltpu.VMEM((s, hd), BF16), pltpu.VMEM((s, hd), BF16),
                        pltpu.VMEM((hd, hd), F32), pltpu.VMEM((1, hd), F32), pltpu.VMEM((1, 1), F32)],
        compiler_params=_params("arbitrary", "arbitrary"),
        name="mlstm_heads",
    )(gate_b, p3, p3, p3, p3, conv_w, conv_w, gates_col, gb_row, gates_row,
      head_norm_g.reshape(2 * nh, 1, hd))


def _ret_kernel(q_ref, k_ref, v_ref, g_ref, cos_ref, sin_ref, ng_ref, y_ref, q_s, k_s, r_s):
    h = pl.program_id(1)
    s = q_ref.shape[0]
    half = HEAD_DIM // 2
    cos = cos_ref[...]
    sin = sin_ref[...]

    def rotate(x_ref, dst, scale):
        x1 = x_ref[:, :half].astype(F32)
        x2 = x_ref[:, half:].astype(F32)
        dst[:, :half] = ((x1 * cos - x2 * sin) * scale).astype(BF16)
        dst[:, half:] = ((x2 * cos + x1 * sin) * scale).astype(BF16)

    rotate(q_ref, q_s, 1.0)
    rotate(k_ref, k_s, HEAD_DIM ** -0.5)
    r_s[...] = jnp.zeros_like(r_s)

    log_gamma = jnp.full((1, 1), math.log(1.0 - 2.0 ** -5.0), F32)
    for i in range(1, N_GROUP_HEADS):
        log_gamma = jnp.where(h == i, math.log(1.0 - 2.0 ** (-5.0 - i)), log_gamma)
    ri = lax.broadcasted_iota(I32, (CHUNK, CHUNK), 0)
    ci = lax.broadcasted_iota(I32, (CHUNK, CHUNK), 1)
    rel = (ri - ci).astype(F32)
    dmask = jnp.where(rel >= 0, jnp.exp(jnp.maximum(rel, 0.0) * log_gamma), 0.0)
    jcol = lax.broadcasted_iota(I32, (CHUNK, 1), 0).astype(F32)
    cross_decay = jnp.exp((jcol + 1.0) * log_gamma)
    state_decay = jnp.exp((CHUNK - 1.0 - jcol) * log_gamma)
    chunk_decay = jnp.exp(CHUNK * log_gamma)

    def body(c, carry):
        st = pl.multiple_of(c * CHUNK, CHUNK)
        qc = q_s[pl.ds(st, CHUNK), :]
        kc = k_s[pl.ds(st, CHUNK), :]
        vc = v_ref[pl.ds(st, CHUNK), :]
        inner = _dot((_dot_nt(qc, kc) * dmask).astype(BF16), vc)
        cross = _dot(qc, r_s[...].astype(BF16)) * cross_decay
        r_s[...] = chunk_decay * r_s[...] + _dot_tn((kc.astype(F32) * state_decay).astype(BF16), vc)
        g = g_ref[pl.ds(st, CHUNK), :].astype(F32)
        y_ref[pl.ds(st, CHUNK), :] = _group_norm_gate(inner + cross, ng_ref[...], g * _sigmoid(g))
        return carry

    lax.fori_loop(0, s // CHUNK, body, 0)


def _retention_heads(p3, cos, sin, head_norm_g):
    bsz, s, _ = p3.shape
    nh = N_GROUP_HEADS
    hd = HEAD_DIM

    def col(off):
        return pl.BlockSpec((None, s, hd), lambda b, h: (b, 0, off + h))

    return pl.pallas_call(
        _ret_kernel,
        grid=(bsz, nh),
        in_specs=[col(4 * nh), col(5 * nh), col(6 * nh), col(7 * nh),
                  pl.BlockSpec((s, hd // 2), lambda b, h: (0, 0)),
                  pl.BlockSpec((s, hd // 2), lambda b, h: (0, 0)),
                  pl.BlockSpec((None, 1, hd), lambda b, h: (nh + h, 0, 0))],
        out_specs=pl.BlockSpec((None, s, hd), lambda b, h: (b, 0, h)),
        out_shape=jax.ShapeDtypeStruct((bsz, s, GROUP_WIDTH), BF16),
        scratch_shapes=[pltpu.VMEM((s, hd), BF16), pltpu.VMEM((s, hd), BF16), pltpu.VMEM((hd, hd), F32)],
        compiler_params=_params("arbitrary", "arbitrary"),
        name="retention_heads",
    )(p3, p3, p3, p3, cos, sin, head_norm_g.reshape(2 * nh, 1, hd))


def _order_key(x):
    bits = pltpu.bitcast(x, I32)
    return bits ^ ((bits >> 31) & 0x7FFFFFFF)


def _dsa_kernel(tab_ref, q_ref, iq_ref, iw_ref, k_ref, v_ref, ika_ref, ikb_ref, qg_ref, kg_ref, o_ref,
                kn_s, ikab_s, vaug_s, key_s, keyt_s, tb_s, wb_s, qall_s, iqall_s, acc_s, m_s, *, topk):
    b = pl.program_id(0)
    qb = pl.program_id(1)
    s = k_ref.shape[0]
    nkb = s // Q_BLOCK
    ri = lax.broadcasted_iota(I32, (Q_BLOCK, Q_BLOCK), 0)
    ci = lax.broadcasted_iota(I32, (Q_BLOCK, Q_BLOCK), 1)

    @pl.when((b == 0) & (qb == 0))
    def _():
        buckets = []
        for back in range(3):
            dist = jnp.maximum(back * Q_BLOCK + ri - ci, 0)
            bucket = jnp.full(dist.shape, REL_BUCKETS // 2, I32)
            for thr in T5_THRESHOLDS:
                bucket = bucket + jnp.where(dist >= thr, 1, 0)
            buckets.append(jnp.where(dist < REL_BUCKETS // 2, dist, bucket))

        def head_body(hh, carry):
            tiles = []
            for bucket in buckets:
                tile = jnp.zeros(bucket.shape, F32)
                for bk in range(REL_BUCKETS):
                    tile = jnp.where(bucket == bk, tab_ref[bk, hh], tile)
                tiles.append(tile)
            for i in range(4):
                tb_s[i, hh, :, :Q_BLOCK] = tiles[min(i, 2)]
                tb_s[i, hh, :, Q_BLOCK:] = tiles[min(max(i - 1, 0), 2)]
            return carry

        lax.fori_loop(0, N_ATTN_HEADS, head_body, 0)

    @pl.when(qb == 0)
    def _():
        k = k_ref[...].astype(F32)
        r = lax.rsqrt(jnp.mean(k * k, axis=-1, keepdims=True) + EPS)
        kn_s[...] = (k * r * kg_ref[...]).astype(BF16)
        for kt in range(nkb):
            ikab_s[kt, :Q_BLOCK, :] = ika_ref[kt * Q_BLOCK:(kt + 1) * Q_BLOCK, :]
            ikab_s[kt, Q_BLOCK:, :] = ikb_ref[kt * Q_BLOCK:(kt + 1) * Q_BLOCK, :]
        vaug_s[:, :ATTN_DIM] = v_ref[...]
        vaug_s[:, ATTN_DIM:] = jnp.ones((s, ATTN_DIM), BF16)

    for hh in range(N_ATTN_HEADS):
        qh = q_ref[:, hh * ATTN_DIM:(hh + 1) * ATTN_DIM].astype(F32)
        r = lax.rsqrt(jnp.mean(qh * qh, axis=-1, keepdims=True) + EPS)
        qall_s[hh * Q_BLOCK:(hh + 1) * Q_BLOCK, :] = (qh * r * qg_ref[...] * (ATTN_DIM ** -0.5)).astype(BF16)
    for hp in range(N_IDX_HEADS // 2):
        iqall_s[hp * Q_BLOCK:(hp + 1) * Q_BLOCK, :] = iq_ref[:, hp * LANES:(hp + 1) * LANES]
    wv = iw_ref[...] * ((N_IDX_HEADS ** -0.5) * (IDX_DIM ** -0.5))
    for hh in range(N_IDX_HEADS):
        wb_s[hh] = jnp.broadcast_to(wv[:, hh:hh + 1], (Q_BLOCK, Q_BLOCK))

    n_tiles = (qb + 2) // 2

    def score_body(kt, carry):
        for half in range(KEY_TILE // Q_BLOCK):
            kb = 2 * kt + half
            r = _dot_nt(iqall_s[...], ikab_s[kb])
            acc = jnp.zeros((Q_BLOCK, Q_BLOCK), F32)
            for hp in range(N_IDX_HEADS // 2):
                blk = r[hp * Q_BLOCK:(hp + 1) * Q_BLOCK]
                acc = acc + jnp.maximum(blk[:, :Q_BLOCK], 0.0) * wb_s[2 * hp]
                acc = acc + jnp.maximum(blk[:, Q_BLOCK:], 0.0) * wb_s[2 * hp + 1]
            causal = (kb * Q_BLOCK + ci) <= (qb * Q_BLOCK + ri)
            key_s[kt, :, half * Q_BLOCK:(half + 1) * Q_BLOCK] = jnp.where(causal, _order_key(acc), INT_MIN)
            causal_t = (kb * Q_BLOCK + ri) <= (qb * Q_BLOCK + ci)
            keyt_s[kt, half * Q_BLOCK:(half + 1) * Q_BLOCK, :] = jnp.where(causal_t, _order_key(acc.T), INT_MIN)
        return carry

    lax.fori_loop(0, n_tiles, score_body, 0)

    def count_ge(cand):
        def inner(kt, acc):
            hit = jnp.where(keyt_s[kt] >= cand, 1.0, 0.0)
            return acc + jnp.sum(hit.reshape(KEY_TILE // COUNT_ROWS, COUNT_ROWS, Q_BLOCK), axis=0)
        acc = lax.fori_loop(0, n_tiles, inner, jnp.zeros((COUNT_ROWS, Q_BLOCK), F32))
        return jnp.sum(acc, axis=0, keepdims=True)

    zero = jnp.zeros((1, Q_BLOCK), I32)
    thr0 = jnp.where(count_ge(zero) >= topk, zero, INT_MIN)

    def bit_body(i, thr):
        cand = thr + jnp.left_shift(jnp.int32(1), 30 - i)
        return jnp.where(count_ge(cand) >= topk, cand, thr)

    thr = lax.fori_loop(0, 31, bit_body, thr0)
    thr = jnp.maximum(thr, INT_MIN + 1)
    thr = jnp.broadcast_to(thr, (Q_BLOCK, Q_BLOCK)).T

    acc_s[...] = jnp.zeros_like(acc_s)
    m_s[...] = jnp.full(m_s.shape, NEG_BIG, F32)

    def attn_body(kt, carry):
        st = pl.multiple_of(kt * KEY_TILE, KEY_TILE)
        mask_bias = jnp.concatenate(
            [jnp.where(key_s[kt, :, i * Q_BLOCK:(i + 1) * Q_BLOCK] >= thr, 0.0, NEG_BIG)
             for i in range(KEY_TILE // Q_BLOCK)], axis=-1)
        back = jnp.minimum(qb - 2 * kt, 3)
        sc = _dot_nt(qall_s[...], kn_s[pl.ds(st, KEY_TILE), :]).reshape(N_ATTN_HEADS, Q_BLOCK, KEY_TILE)
        sc = sc + tb_s[back] + mask_bias[None]
        m_prev = m_s[...]
        m_new = jnp.maximum(m_prev, jnp.max(sc, axis=-1, keepdims=True))
        alpha = jnp.exp(m_prev - m_new).reshape(N_ATTN_HEADS * Q_BLOCK, ATTN_DIM)
        p = jnp.concatenate([jnp.exp(sc[:, :, i * ATTN_DIM:(i + 1) * ATTN_DIM] - m_new)
                             for i in range(KEY_TILE // ATTN_DIM)], axis=-1)
        p = p.reshape(N_ATTN_HEADS * Q_BLOCK, KEY_TILE)
        pv = _dot(p.astype(BF16), vaug_s[pl.ds(st, KEY_TILE), :])
        acc_s[:, :ATTN_DIM] = alpha * acc_s[:, :ATTN_DIM] + pv[:, :ATTN_DIM]
        acc_s[:, ATTN_DIM:] = alpha * acc_s[:, ATTN_DIM:] + pv[:, ATTN_DIM:]
        m_s[...] = m_new
        return carry

    lax.fori_loop(0, n_tiles, attn_body, 0)

    out = acc_s[:, :ATTN_DIM] / acc_s[:, ATTN_DIM:]
    for hh in range(N_ATTN_HEADS):
        o_ref[:, hh * ATTN_DIM:(hh + 1) * ATTN_DIM] = out[hh * Q_BLOCK:(hh + 1) * Q_BLOCK].astype(BF16)


def _dsa_attention(p3, iw3, q_norm_g, k_norm_g, rel_bias):
    bsz, s, _ = p3.shape
    nq = s // Q_BLOCK
    qw = N_ATTN_HEADS * ATTN_DIM
    iqw = N_IDX_HEADS * IDX_DIM
    small0 = (qw + iqw) // LANES

    def small(off):
        return pl.BlockSpec((None, s, LANES), lambda b, i: (b, 0, small0 + off))

    return pl.pallas_call(
        functools.partial(_dsa_kernel, topk=min(MAX_TOPK, s // 4)),
        grid=(bsz, nq),
        in_specs=[pl.BlockSpec(memory_space=pltpu.SMEM),
                  pl.BlockSpec((None, Q_BLOCK, qw), lambda b, i: (b, i, 0)),
                  pl.BlockSpec((None, Q_BLOCK, iqw), lambda b, i: (b, i, qw // iqw)),
                  pl.BlockSpec((None, Q_BLOCK, LANES), lambda b, i: (b, i, 0)),
                  small(0), small(1), small(2), small(3),
                  pl.BlockSpec((1, ATTN_DIM), lambda b, i: (0, 0)),
                  pl.BlockSpec((1, ATTN_DIM), lambda b, i: (0, 0))],
        out_specs=pl.BlockSpec((None, Q_BLOCK, qw), lambda b, i: (b, i, 0)),
        out_shape=jax.ShapeDtypeStruct((bsz, s, qw), BF16),
        scratch_shapes=[pltpu.VMEM((s, ATTN_DIM), BF16),
                        pltpu.VMEM((nq, 2 * Q_BLOCK, LANES), BF16),
                        pltpu.VMEM((s, 2 * ATTN_DIM), BF16),
                        pltpu.VMEM((s // KEY_TILE, Q_BLOCK, KEY_TILE), I32),
                        pltpu.VMEM((s // KEY_TILE, KEY_TILE, Q_BLOCK), I32),
                        pltpu.VMEM((4, N_ATTN_HEADS, Q_BLOCK, KEY_TILE), F32),
                        pltpu.VMEM((N_IDX_HEADS, Q_BLOCK, Q_BLOCK), F32),
                        pltpu.VMEM((N_ATTN_HEADS * Q_BLOCK, ATTN_DIM), BF16),
                        pltpu.VMEM((N_IDX_HEADS // 2 * Q_BLOCK, LANES), BF16),
                        pltpu.VMEM((N_ATTN_HEADS * Q_BLOCK, 2 * ATTN_DIM), F32),
                        pltpu.VMEM((N_ATTN_HEADS, Q_BLOCK, ATTN_DIM), F32)],
        compiler_params=_params("arbitrary", "arbitrary"),
        name="dsa_attention",
    )(rel_bias, p3, p3, iw3, p3, p3, p3, p3, q_norm_g.reshape(1, ATTN_DIM), k_norm_g.reshape(1, ATTN_DIM))


def _outproj_kernel(ya_ref, yb_ref, x_ref, w_ref, g1_ref, n2_ref, sc2_ref, sh2_ref, x1_ref, h2_ref):
    half = ya_ref.shape[1]
    mix = _dot(ya_ref[...], w_ref[:half, :]) + _dot(yb_ref[...], w_ref[half:, :])
    x1 = x_ref[...] + g1_ref[...] * mix
    x1_ref[...] = x1
    h2_ref[...] = _norm_modulate(x1, n2_ref[...], sc2_ref[...], sh2_ref[...]).astype(BF16)


def _out_projection(ya, yb, yb_col, x2d, w_out, norm2_g, mod3, layer, bsz, tm):
    m, d = x2d.shape
    tpb = (m // bsz) // tm
    row = pl.BlockSpec((tm, d), lambda i: (i, 0))
    return pl.pallas_call(
        _outproj_kernel,
        grid=(m // tm,),
        in_specs=[pl.BlockSpec((tm, d // 2), lambda i: (i, 0)),
                  pl.BlockSpec((tm, d // 2), lambda i: (i, yb_col)),
                  row,
                  pl.BlockSpec((d, d), lambda i: (0, 0)),
                  _mod_spec(layer, 2, bsz, tpb),
                  pl.BlockSpec((1, d), lambda i: (0, 0)),
                  _mod_spec(layer, 4, bsz, tpb),
                  _mod_spec(layer, 3, bsz, tpb)],
        out_specs=[row, row],
        out_shape=[jax.ShapeDtypeStruct((m, d), F32), jax.ShapeDtypeStruct((m, d), BF16)],
        compiler_params=_params("arbitrary"),
        name=f"out_projection_{layer}",
    )(ya, yb, x2d, w_out, mod3, norm2_g.reshape(1, d), mod3, mod3)


def _mlp_kernel(h_ref, x1_ref, w1_ref, w2_ref, g2_ref, o_ref):
    j = pl.program_id(1)
    a = jnp.square(jnp.maximum(_dot(h_ref[...], w1_ref[...]), 0.0)).astype(BF16)
    part = _dot(a, w2_ref[...])

    @pl.when(j == 0)
    def _():
        o_ref[...] = part

    @pl.when(j > 0)
    def _():
        o_ref[...] += part

    @pl.when(j == pl.num_programs(1) - 1)
    def _():
        o_ref[...] = x1_ref[...] + g2_ref[...] * o_ref[...]


def _mlp(h2, x1, w1, w2, mod3, layer, bsz, tm, tf):
    m, d = x1.shape
    ff = w1.shape[1]
    tpb = (m // bsz) // tm
    row = pl.BlockSpec((tm, d), lambda i, j: (i, 0))
    return pl.pallas_call(
        _mlp_kernel,
        grid=(m // tm, ff // tf),
        in_specs=[row, row,
                  pl.BlockSpec((d, tf), lambda i, j: (0, j)),
                  pl.BlockSpec((tf, d), lambda i, j: (j, 0)),
                  _mod_spec(layer, 5, bsz, tpb)],
        out_specs=row,
        out_shape=jax.ShapeDtypeStruct((m, d), F32),
        compiler_params=_params("arbitrary", "arbitrary"),
        name=f"mlp_{layer}",
    )(h2, x1, w1, w2, mod3)


def _pad_cols(w, width):
    return jnp.pad(w, ((0, 0), (0, width - w.shape[1])))


def _even_weights(w_in):
    gw = GROUP_WIDTH
    g0 = 4 * gw
    g1 = g0 + 2 * N_GROUP_HEADS
    main = jnp.concatenate([w_in[:, :g0], w_in[:, g1:]], axis=1).astype(BF16)
    tail = _pad_cols(w_in[:, g0:g1], LANES).astype(BF16)
    return main, tail


def _odd_weights(w_in):
    qw = N_ATTN_HEADS * ATTN_DIM
    iqw = N_IDX_HEADS * IDX_DIM
    o_k, o_v, o_iq = qw, qw + ATTN_DIM, qw + 2 * ATTN_DIM
    o_ik = o_iq + iqw
    o_iw = o_ik + IDX_DIM
    w_ik = w_in[:, o_ik:o_iw]
    zeros = jnp.zeros_like(w_ik)
    main = jnp.concatenate([w_in[:, :qw], w_in[:, o_iq:o_ik], w_in[:, o_k:o_v], w_in[:, o_v:o_iq],
                            w_ik, zeros, zeros, w_ik], axis=1).astype(BF16)
    tail = _pad_cols(w_in[:, o_iw:], LANES).astype(BF16)
    return main, tail


def kernel(x, c, ada_w, ada_b, norm1_g, norm2_g, mlp_w1, mlp_w2, even_w_in, even_conv_w, even_gate_b,
           even_head_norm_g, even_w_out, odd_w_in, odd_q_norm_g, odd_k_norm_g, odd_w_out, rel_bias):
    bsz, s, d = x.shape
    depth = ada_w.shape[0]
    m = bsz * s
    tm = 512
    mod3 = _ada_modulation(c, ada_w, ada_b).reshape(depth * bsz * 6, 1, d)
    cos, sin = _rope_tables(s)
    xc = x.reshape(m, d)
    for l in range(depth):
        e = l // 2
        if l % 2 == 0:
            w_main, w_tail = _even_weights(even_w_in[e])
            p, gates = _in_projection(xc, norm1_g[l], mod3, l, bsz, w_main, w_tail, tm, 1024)
            p3 = p.reshape(bsz, s, -1)
            gates_col = gates.reshape(bsz, s, LANES)
            gates_row = jnp.swapaxes(
                gates_col[:, :, :2 * N_GROUP_HEADS].reshape(bsz, s // CHUNK, CHUNK, 2 * N_GROUP_HEADS), 2, 3)
            ya = _mlstm_heads(p3, gates_col, gates_row, even_conv_w[e], even_gate_b[e], even_head_norm_g[e])
            yb = _retention_heads(p3, cos, sin, even_head_norm_g[e])
            ya, yb, yb_col = ya.reshape(m, d // 2), yb.reshape(m, d // 2), 0
            w_out = even_w_out[e]
        else:
            w_main, w_tail = _odd_weights(odd_w_in[e])
            p, iw = _in_projection(xc, norm1_g[l], mod3, l, bsz, w_main, w_tail, tm, 512)
            y = _dsa_attention(p.reshape(bsz, s, -1), iw.reshape(bsz, s, LANES),
                               odd_q_norm_g[e], odd_k_norm_g[e], rel_bias)
            ya = yb = y.reshape(m, d)
            yb_col = 1
            w_out = odd_w_out[e]
        x1, h2 = _out_projection(ya, yb, yb_col, xc, w_out.astype(BF16), norm2_g[l], mod3, l, bsz, tm)
        xc = _mlp(h2, x1, mlp_w1[l].astype(BF16), mlp_w2[l].astype(BF16), mod3, l, bsz, tm, 512)
    return xc.reshape(bsz, s, d)
```

```python
import functools
import math

import numpy as np
import jax
import jax.numpy as jnp
from jax import lax
from jax.experimental import pallas as pl
from jax.experimental.pallas import tpu as pltpu

F32 = jnp.float32
BF16 = jnp.bfloat16
I32 = jnp.int32

D_MODEL = 2048
D_FF = 4 * D_MODEL
EPS = 1e-6
CHUNK = 128
HEAD_DIM = 256
N_GROUP_HEADS = 4
MLSTM_HEADS_PER_STEP = 2
RET_HEADS_PER_STEP = 4
GROUP_WIDTH = N_GROUP_HEADS * HEAD_DIM
CONV_WIDTH = 4
GATE_SOFTCAP = 15.0
ROPE_BASE = 10000.0
N_ATTN_HEADS = 16
ATTN_DIM = 128
N_IDX_HEADS = 16
IDX_DIM = 64
MAX_TOPK = 256
Q_BLOCK = 128
KEY_TILE = 2 * Q_BLOCK
REL_BUCKETS = 32
REL_MAX_DISTANCE = 128
LANES = 128
COUNT_ROWS = 64
INT_MIN = -(2 ** 31)
NEG_BIG = -1e30
LOG2E = math.log2(math.e)
VMEM_LIMIT = 56 * 1024 * 1024
ROW_TILE = 512
PROJ_ROW_TILE = 1024
FF_TILE = 1024


def _t5_large_bucket_thresholds():
    max_exact = REL_BUCKETS // 2
    d = np.arange(0, 4 * REL_MAX_DISTANCE)
    large = max_exact + (np.log(np.maximum(d, 1) / max_exact) / math.log(REL_MAX_DISTANCE / max_exact)
                         * (REL_BUCKETS - max_exact)).astype(np.int64)
    bucket = np.where(d < max_exact, d, np.minimum(large, REL_BUCKETS - 1))
    return [int(d[bucket >= b].min()) for b in range(max_exact + 1, REL_BUCKETS)]


T5_THRESHOLDS = _t5_large_bucket_thresholds()
assert T5_THRESHOLDS[-1] <= Q_BLOCK + 1


def _params(*semantics):
    return pltpu.CompilerParams(dimension_semantics=semantics, vmem_limit_bytes=VMEM_LIMIT)


def _dot(a, b):
    return jnp.dot(a, b, preferred_element_type=F32)


def _dot_nt(a, b):
    return lax.dot_general(a, b, (((1,), (1,)), ((), ())), preferred_element_type=F32)


def _dot_tn(a, b):
    return lax.dot_general(a, b, (((0,), (0,)), ((), ())), preferred_element_type=F32)


def _sigmoid(x):
    return 1.0 / (1.0 + jnp.exp(-x))


def _log_sigmoid(x):
    return jnp.minimum(x, 0.0) - jnp.log1p(jnp.exp(-jnp.abs(x)))


def _softcap(x):
    return GATE_SOFTCAP * jnp.tanh(x / GATE_SOFTCAP)


def _norm_modulate(x, g, sc, sh):
    r = lax.rsqrt(jnp.mean(x * x, axis=-1, keepdims=True) + EPS)
    return (x * r * g) * (1.0 + sc) + sh


def _ada_kernel(c_ref, w_ref, b_ref, o_ref):
    c = c_ref[...]
    cond = c * _sigmoid(c)
    o_ref[...] = jnp.dot(cond, w_ref[...], preferred_element_type=F32,
                         precision=lax.Precision.HIGHEST) + b_ref[...]


def _ada_modulation(c, ada_w, ada_b):
    depth, d, n = ada_w.shape
    bsz = c.shape[0]
    tn = 1024
    return pl.pallas_call(
        _ada_kernel,
        grid=(depth, n // tn),
        in_specs=[pl.BlockSpec((bsz, d), lambda l, j: (0, 0)),
                  pl.BlockSpec((None, d, tn), lambda l, j: (l, 0, j)),
                  pl.BlockSpec((None, 1, tn), lambda l, j: (l, 0, j))],
        out_specs=pl.BlockSpec((None, bsz, tn), lambda l, j: (l, 0, j)),
        out_shape=jax.ShapeDtypeStruct((depth, bsz, n), F32),
        compiler_params=_params("arbitrary", "arbitrary"),
        name="ada_modulation",
    )(c, ada_w, ada_b.reshape(depth, 1, n))


def _mod_spec(layer, which, bsz, tiles_per_batch):
    return pl.BlockSpec((None, 1, D_MODEL),
                        lambda i, *_: ((layer * bsz + i // tiles_per_batch) * 6 + which, 0, 0))


def _proj_kernel(x_ref, g_ref, sc_ref, sh_ref, w_ref, wt_ref, p_ref, t_ref, h_s):
    @pl.when(pl.program_id(1) == 0)
    def _():
        h = _norm_modulate(x_ref[...], g_ref[...], sc_ref[...], sh_ref[...]).astype(BF16)
        h_s[...] = h
        t_ref[...] = _dot(h, wt_ref[...])

    p_ref[...] = _dot(h_s[...], w_ref[...]).astype(p_ref.dtype)


def _in_projection(x2d, norm_g, mod3, layer, bsz, w_main, w_tail, tm, tn):
    m, d = x2d.shape
    n = w_main.shape[1]
    tpb = (m // bsz) // tm
    return pl.pallas_call(
        _proj_kernel,
        grid=(m // tm, n // tn),
        in_specs=[pl.BlockSpec((tm, d), lambda i, j: (i, 0)),
                  pl.BlockSpec((1, d), lambda i, j: (0, 0)),
                  _mod_spec(layer, 1, bsz, tpb),
                  _mod_spec(layer, 0, bsz, tpb),
                  pl.BlockSpec((d, tn), lambda i, j: (0, j)),
                  pl.BlockSpec((d, LANES), lambda i, j: (0, 0))],
        out_specs=[pl.BlockSpec((tm, tn), lambda i, j: (i, j)),
                   pl.BlockSpec((tm, LANES), lambda i, j: (i, 0))],
        out_shape=[jax.ShapeDtypeStruct((m, n), BF16),
                   jax.ShapeDtypeStruct((m, LANES), F32)],
        scratch_shapes=[pltpu.VMEM((tm, d), BF16)],
        compiler_params=_params("arbitrary", "arbitrary"),
        name=f"in_projection_{layer}",
    )(x2d, norm_g.reshape(1, d), mod3, mod3, w_main, w_tail)


def _rope_kernel(cos_ref, sin_ref):
    s, half = cos_ref.shape
    pos = lax.broadcasted_iota(I32, (s, half), 0).astype(F32)
    idx = lax.broadcasted_iota(I32, (s, half), 1).astype(F32)
    inv = jnp.exp(idx * (-math.log(ROPE_BASE) / half))
    ang = pos * inv
    cos_ref[...] = jnp.cos(ang)
    sin_ref[...] = jnp.sin(ang)


def _rope_tables(s):
    half = HEAD_DIM // 2
    return pl.pallas_call(
        _rope_kernel,
        out_shape=[jax.ShapeDtypeStruct((s, half), F32)] * 2,
        name="rope_tables",
    )()


def _group_norm_gate(h, ng, gate):
    r = lax.rsqrt(jnp.mean(h * h, axis=-1, keepdims=True) + EPS)
    return (h * r * ng * gate).astype(BF16)


def _mlstm_kernel(gb_ref, q_ref, k_ref, v_ref, o_ref, cwq_ref, cwk_ref, gc_ref, gbrow_ref, gr_ref,
                  ng_ref, y_ref, q_s, k_s, c_s, n_s, m_s):
    hps = ng_ref.shape[0]
    h0 = pl.program_id(1) * hps
    s = q_ref.shape[0]
    rows = lax.broadcasted_iota(I32, (s, 1), 0)

    def conv_silu(x_ref, cw_ref):
        x = x_ref[...].astype(F32)
        acc = x * cw_ref[CONV_WIDTH - 1:CONV_WIDTH, :]
        for sft in range(1, CONV_WIDTH):
            xs = jnp.where(rows >= sft, pltpu.roll(x, sft, 0), 0.0)
            acc = acc + xs * cw_ref[CONV_WIDTH - 1 - sft:CONV_WIDTH - sft, :]
        return acc * _sigmoid(acc)

    q_s[...] = conv_silu(q_ref, cwq_ref).astype(BF16)
    k_s[...] = (conv_silu(k_ref, cwk_ref) * (HEAD_DIM ** -0.5)).astype(BF16)
    c_s[...] = jnp.zeros_like(c_s)
    n_s[...] = jnp.zeros_like(n_s)
    m_s[...] = jnp.zeros_like(m_s)

    lane = lax.broadcasted_iota(I32, (CHUNK, LANES), 1)
    ri = lax.broadcasted_iota(I32, (CHUNK, CHUNK), 0)
    ci = lax.broadcasted_iota(I32, (CHUNK, CHUNK), 1)
    tril = ri >= ci

    def head_chunk(c, st, gcap, j):
        h = h0 + j
        cols = slice(j * HEAD_DIM, (j + 1) * HEAD_DIM)
        li_col = jnp.sum(jnp.where(lane == h, gcap, 0.0), axis=1, keepdims=True)
        lf_col = jnp.sum(jnp.where(lane == h + N_GROUP_HEADS, _log_sigmoid(gcap), 0.0), axis=1, keepdims=True)
        li_row = _softcap(gr_ref[c, pl.ds(h, 1), :] + gb_ref[h])
        lf_row = _log_sigmoid(_softcap(gr_ref[c, pl.ds(h + N_GROUP_HEADS, 1), :] + gb_ref[h + N_GROUP_HEADS]))
        bcum_col = jnp.sum(jnp.where(tril, lf_row, 0.0), axis=1, keepdims=True)
        bcum_row = jnp.sum(jnp.where(ri <= ci, lf_col, 0.0), axis=0, keepdims=True)
        b_last = jnp.sum(lf_row, axis=1, keepdims=True)
        m_prev = m_s[j]

        dmat = jnp.where(tril, bcum_col - bcum_row + li_row, -jnp.inf)
        m_inter = bcum_col + m_prev
        m_row = jnp.maximum(jnp.max(dmat, axis=1, keepdims=True), m_inter)
        qc = q_s[pl.ds(st, CHUNK), cols]
        kc = k_s[pl.ds(st, CHUNK), cols]
        vc = v_ref[pl.ds(st, CHUNK), cols]
        sc = _dot_nt(qc, kc) * jnp.exp(dmat - m_row)
        inter = jnp.exp(m_inter - m_row)
        num = _dot(sc.astype(BF16), vc) + inter * _dot(qc, c_s[j].astype(BF16))
        den = (jnp.sum(sc, axis=1, keepdims=True)
               + inter * jnp.sum(qc.astype(F32) * n_s[j], axis=1, keepdims=True))
        h_out = num / jnp.maximum(jnp.abs(den), jnp.exp(-m_row))

        g_col = b_last - bcum_col + li_col
        m_new = jnp.maximum(b_last + m_prev, jnp.max(g_col, axis=0, keepdims=True))
        decay = jnp.exp(b_last + m_prev - m_new)
        kw = kc.astype(F32) * jnp.exp(g_col - m_new)
        c_s[j] = decay * c_s[j] + _dot_tn(kw.astype(BF16), vc)
        n_s[j] = decay * n_s[j] + jnp.sum(kw, axis=0, keepdims=True)
        m_s[j] = m_new

        gate = _sigmoid(o_ref[pl.ds(st, CHUNK), cols].astype(F32))
        y_ref[pl.ds(st, CHUNK), cols] = _group_norm_gate(h_out, ng_ref[j], gate)

    def body(c, carry):
        st = pl.multiple_of(c * CHUNK, CHUNK)
        gcap = _softcap(gc_ref[pl.ds(st, CHUNK), :] + gbrow_ref[...])
        for j in range(hps):
            head_chunk(c, st, gcap, j)
        return carry

    lax.fori_loop(0, s // CHUNK, body, 0)


def _mlstm_heads(p3, gates_col, gates_row, conv_w, gate_b, head_norm_g):
    bsz, s, _ = p3.shape
    nh = N_GROUP_HEADS
    hps = MLSTM_HEADS_PER_STEP
    nsteps = nh // hps
    w = hps * HEAD_DIM

    def col(group):
        return pl.BlockSpec((None, s, w), lambda b, h: (b, 0, group * nsteps + h))

    gb_row = jnp.zeros((1, LANES), F32).at[0, :2 * nh].set(gate_b)
    return pl.pallas_call(
        _mlstm_kernel,
        grid=(bsz, nsteps),
        in_specs=[pl.BlockSpec(memory_space=pltpu.SMEM),
                  col(0), col(1), col(2), col(3),
                  pl.BlockSpec((CONV_WIDTH, w), lambda b, h: (0, h)),
                  pl.BlockSpec((CONV_WIDTH, w), lambda b, h: (0, nsteps + h)),
                  pl.BlockSpec((None, s, LANES), lambda b, h: (b, 0, 0)),
                  pl.BlockSpec((1, LANES), lambda b, h: (0, 0)),
                  pl.BlockSpec((None, s // CHUNK, 2 * nh, CHUNK), lambda b, h: (b, 0, 0, 0)),
                  pl.BlockSpec((hps, 1, HEAD_DIM), lambda b, h: (h, 0, 0))],
        out_specs=pl.BlockSpec((None, s, w), lambda b, h: (b, 0, h)),
        out_shape=jax.ShapeDtypeStruct((bsz, s, GROUP_WIDTH), BF16),
        scratch_shapes=[pltpu.VMEM((s, w), BF16), pltpu.VMEM((s, w), BF16),
                        pltpu.VMEM((hps, HEAD_DIM, HEAD_DIM), F32), pltpu.VMEM((hps, 1, HEAD_DIM), F32),
                        pltpu.VMEM((hps, 1, 1), F32)],
        compiler_params=_params("arbitrary", "arbitrary"),
        name="mlstm_heads",
    )(gate_b, p3, p3, p3, p3, conv_w, conv_w, gates_col, gb_row, gates_row,
      head_norm_g.reshape(2 * nh, 1, HEAD_DIM))


def _ret_kernel(q_ref, k_ref, v_ref, g_ref, cos_ref, sin_ref, ng_ref, y_ref, q_s, k_s, r_s):
    hps = ng_ref.shape[0]
    h0 = pl.program_id(1) * hps
    s = q_ref.shape[0]
    half = HEAD_DIM // 2
    cos = cos_ref[...]
    sin = sin_ref[...]

    def rotate(x_ref, dst, scale):
        for j in range(hps):
            lo = slice(j * HEAD_DIM, j * HEAD_DIM + half)
            hi = slice(j * HEAD_DIM + half, (j + 1) * HEAD_DIM)
            x1 = x_ref[:, lo].astype(F32)
            x2 = x_ref[:, hi].astype(F32)
            dst[:, lo] = ((x1 * cos - x2 * sin) * scale).astype(BF16)
            dst[:, hi] = ((x2 * cos + x1 * sin) * scale).astype(BF16)

    rotate(q_ref, q_s, 1.0)
    rotate(k_ref, k_s, HEAD_DIM ** -0.5)
    r_s[...] = jnp.zeros_like(r_s)

    ri = lax.broadcasted_iota(I32, (CHUNK, CHUNK), 0)
    ci = lax.broadcasted_iota(I32, (CHUNK, CHUNK), 1)
    rel = (ri - ci).astype(F32)
    jcol = lax.broadcasted_iota(I32, (CHUNK, 1), 0).astype(F32)
    decays = []
    for j in range(hps):
        log_gamma = jnp.full((1, 1), math.log(1.0 - 2.0 ** -5.0), F32)
        for i in range(1, N_GROUP_HEADS):
            log_gamma = jnp.where(h0 + j == i, math.log(1.0 - 2.0 ** (-5.0 - i)), log_gamma)
        decays.append((jnp.where(rel >= 0, jnp.exp(jnp.maximum(rel, 0.0) * log_gamma), 0.0),
                       jnp.exp((jcol + 1.0) * log_gamma),
                       jnp.exp((CHUNK - 1.0 - jcol) * log_gamma),
                       jnp.exp(CHUNK * log_gamma)))

    def body(c, carry):
        st = pl.multiple_of(c * CHUNK, CHUNK)
        for j in range(hps):
            dmask, cross_decay, state_decay, chunk_decay = decays[j]
            cols = slice(j * HEAD_DIM, (j + 1) * HEAD_DIM)
            qc = q_s[pl.ds(st, CHUNK), cols]
            kc = k_s[pl.ds(st, CHUNK), cols]
            vc = v_ref[pl.ds(st, CHUNK), cols]
            inner = _dot((_dot_nt(qc, kc) * dmask).astype(BF16), vc)
            cross = _dot(qc, r_s[j].astype(BF16)) * cross_decay
            r_s[j] = chunk_decay * r_s[j] + _dot_tn((kc.astype(F32) * state_decay).astype(BF16), vc)
            g = g_ref[pl.ds(st, CHUNK), cols].astype(F32)
            y_ref[pl.ds(st, CHUNK), cols] = _group_norm_gate(inner + cross, ng_ref[j], g * _sigmoid(g))
        return carry

    lax.fori_loop(0, s // CHUNK, body, 0)


def _retention_heads(p3, cos, sin, head_norm_g):
    bsz, s, _ = p3.shape
    nh = N_GROUP_HEADS
    hps = RET_HEADS_PER_STEP
    nsteps = nh // hps
    w = hps * HEAD_DIM

    def col(group):
        return pl.BlockSpec((None, s, w), lambda b, h: (b, 0, group * nsteps + h))

    return pl.pallas_call(
        _ret_kernel,
        grid=(bsz, nsteps),
        in_specs=[col(4), col(5), col(6), col(7),
                  pl.BlockSpec((s, HEAD_DIM // 2), lambda b, h: (0, 0)),
                  pl.BlockSpec((s, HEAD_DIM // 2), lambda b, h: (0, 0)),
                  pl.BlockSpec((hps, 1, HEAD_DIM), lambda b, h: (nsteps + h, 0, 0))],
        out_specs=pl.BlockSpec((None, s, w), lambda b, h: (b, 0, h)),
        out_shape=jax.ShapeDtypeStruct((bsz, s, GROUP_WIDTH), BF16),
        scratch_shapes=[pltpu.VMEM((s, w), BF16), pltpu.VMEM((s, w), BF16),
                        pltpu.VMEM((hps, HEAD_DIM, HEAD_DIM), F32)],
        compiler_params=_params("arbitrary", "arbitrary"),
        name="retention_heads",
    )(p3, p3, p3, p3, cos, sin, head_norm_g.reshape(2 * nh, 1, HEAD_DIM))


def _order_key(x):
    bits = pltpu.bitcast(x, I32)
    return bits ^ ((bits >> 31) & 0x7FFFFFFF)


def _dsa_kernel(tab_ref, q_ref, iq_ref, iw_ref, k_ref, v_ref, ika_ref, ikb_ref, qg_ref, kg_ref, o_ref,
                kext_s, ikab_s, vaug_s, keyt_s, tb_s, wb_s, qall_s, iqall_s, acc_s, m_s, *, topk):
    b = pl.program_id(0)
    qb = pl.program_id(1)
    s = k_ref.shape[0]
    nkb = s // Q_BLOCK
    ri = lax.broadcasted_iota(I32, (Q_BLOCK, Q_BLOCK), 0)
    ci = lax.broadcasted_iota(I32, (Q_BLOCK, Q_BLOCK), 1)

    @pl.when((b == 0) & (qb == 0))
    def _():
        buckets = []
        for back in range(2):
            dist = jnp.maximum(back * Q_BLOCK + ri - ci, 0)
            bucket = jnp.full(dist.shape, REL_BUCKETS // 2, I32)
            for thr in T5_THRESHOLDS:
                bucket = bucket + jnp.where(dist >= thr, 1, 0)
            buckets.append(jnp.where(dist < REL_BUCKETS // 2, dist, bucket))

        def head_body(hh, carry):
            far = tab_ref[REL_BUCKETS - 1, hh]
            tiles = []
            for bucket in buckets:
                tile = jnp.zeros(bucket.shape, F32)
                for bk in range(REL_BUCKETS):
                    tile = jnp.where(bucket == bk, (tab_ref[bk, hh] - far) * LOG2E, tile)
                tiles.append(tile)
            zeros = jnp.zeros((Q_BLOCK, Q_BLOCK), F32)
            for i, (first, second) in enumerate(((tiles[0], zeros), (tiles[1], tiles[0]), (zeros, tiles[1]))):
                tb_s[i, hh, :, :Q_BLOCK] = first
                tb_s[i, hh, :, Q_BLOCK:] = second
            return carry

        lax.fori_loop(0, N_ATTN_HEADS, head_body, 0)
        onehot = jnp.where(ri == ci, 1.0, 0.0).astype(BF16)
        for hh in range(N_ATTN_HEADS):
            qall_s[hh * Q_BLOCK:(hh + 1) * Q_BLOCK, ATTN_DIM:] = onehot

    @pl.when(qb == 0)
    def _():
        k = k_ref[...].astype(F32)
        r = lax.rsqrt(jnp.mean(k * k, axis=-1, keepdims=True) + EPS)
        kext_s[:, :ATTN_DIM] = (k * r * kg_ref[...]).astype(BF16)
        for kt in range(nkb):
            ikab_s[kt, :Q_BLOCK, :] = ika_ref[kt * Q_BLOCK:(kt + 1) * Q_BLOCK, :]
            ikab_s[kt, Q_BLOCK:, :] = ikb_ref[kt * Q_BLOCK:(kt + 1) * Q_BLOCK, :]
        vaug_s[:, :ATTN_DIM] = v_ref[...]
        vaug_s[:, ATTN_DIM:] = jnp.ones((s, ATTN_DIM), BF16)

    for hh in range(N_ATTN_HEADS):
        qh = q_ref[:, hh * ATTN_DIM:(hh + 1) * ATTN_DIM].astype(F32)
        r = lax.rsqrt(jnp.mean(qh * qh, axis=-1, keepdims=True) + EPS)
        qall_s[hh * Q_BLOCK:(hh + 1) * Q_BLOCK, :ATTN_DIM] = (
            qh * r * qg_ref[...] * (ATTN_DIM ** -0.5 * LOG2E)).astype(BF16)
    for hp in range(N_IDX_HEADS // 2):
        iqall_s[hp * Q_BLOCK:(hp + 1) * Q_BLOCK, :] = iq_ref[:, hp * LANES:(hp + 1) * LANES]
    wv = iw_ref[...] * ((N_IDX_HEADS ** -0.5) * (IDX_DIM ** -0.5))
    for hh in range(N_IDX_HEADS):
        wb_s[hh] = jnp.broadcast_to(wv[:, hh:hh + 1], (Q_BLOCK, Q_BLOCK))

    n_tiles = (qb + 2) // 2

    def score_body(kt, carry):
        for half in range(KEY_TILE // Q_BLOCK):
            kb = 2 * kt + half
            r = _dot_nt(iqall_s[...], ikab_s[kb])
            acc = jnp.zeros((Q_BLOCK, Q_BLOCK), F32)
            for hp in range(N_IDX_HEADS // 2):
                blk = r[hp * Q_BLOCK:(hp + 1) * Q_BLOCK]
                acc = acc + jnp.maximum(blk[:, :Q_BLOCK], 0.0) * wb_s[2 * hp]
                acc = acc + jnp.maximum(blk[:, Q_BLOCK:], 0.0) * wb_s[2 * hp + 1]
            causal_t = (kb * Q_BLOCK + ri) <= (qb * Q_BLOCK + ci)
            keyt_s[kt, half * Q_BLOCK:(half + 1) * Q_BLOCK, :] = jnp.where(causal_t, _order_key(acc.T), INT_MIN)
        return carry

    lax.fori_loop(0, n_tiles, score_body, 0)

    def count_ge(cand):
        def inner(kt, acc):
            hit = jnp.where(keyt_s[kt] >= cand, 1.0, 0.0)
            return acc + jnp.sum(hit.reshape(KEY_TILE // COUNT_ROWS, COUNT_ROWS, Q_BLOCK), axis=0)
        acc = lax.fori_loop(0, n_tiles, inner, jnp.zeros((COUNT_ROWS, Q_BLOCK), F32))
        return jnp.sum(acc, axis=0, keepdims=True)

    zero = jnp.zeros((1, Q_BLOCK), I32)
    thr0 = jnp.where(count_ge(zero) >= topk, zero, INT_MIN)

    def bit_body(i, thr):
        cand = thr + jnp.left_shift(jnp.int32(1), 30 - i)
        return jnp.where(count_ge(cand) >= topk, cand, thr)

    thr = lax.fori_loop(0, 31, bit_body, thr0)
    thr = jnp.maximum(thr, INT_MIN + 1)

    acc_s[...] = jnp.zeros_like(acc_s)
    m_s[...] = jnp.full(m_s.shape, NEG_BIG, F32)

    def attn_tile(kt, bias):
        st = pl.multiple_of(kt * KEY_TILE, KEY_TILE)
        kext_s[pl.ds(st, KEY_TILE), ATTN_DIM:] = jnp.where(keyt_s[kt] >= thr, 0.0, NEG_BIG).astype(BF16)
        sc = _dot_nt(qall_s[...], kext_s[pl.ds(st, KEY_TILE), :]).reshape(N_ATTN_HEADS, Q_BLOCK, KEY_TILE)
        if bias is not None:
            sc = sc + bias
        m_prev = m_s[...]
        m_new = jnp.maximum(m_prev, jnp.max(sc, axis=-1, keepdims=True))
        alpha = jnp.exp2(m_prev - m_new).reshape(N_ATTN_HEADS * Q_BLOCK, ATTN_DIM)
        p = jnp.concatenate([jnp.exp2(sc[:, :, i * ATTN_DIM:(i + 1) * ATTN_DIM] - m_new)
                             for i in range(KEY_TILE // ATTN_DIM)], axis=-1)
        p = p.reshape(N_ATTN_HEADS * Q_BLOCK, KEY_TILE)
        pv = _dot(p.astype(BF16), vaug_s[pl.ds(st, KEY_TILE), :])
        acc_s[:, :ATTN_DIM] = alpha * acc_s[:, :ATTN_DIM] + pv[:, :ATTN_DIM]
        acc_s[:, ATTN_DIM:] = alpha * acc_s[:, ATTN_DIM:] + pv[:, ATTN_DIM:]
        m_s[...] = m_new

    def far_body(kt, carry):
        attn_tile(kt, None)
        return carry

    def near_body(kt, carry):
        attn_tile(kt, tb_s[qb - 2 * kt])
        return carry

    n_far = jnp.maximum((qb - 1) // 2, 0)
    lax.fori_loop(0, n_far, far_body, 0)
    lax.fori_loop(n_far, n_tiles, near_body, 0)

    out = acc_s[:, :ATTN_DIM] / acc_s[:, ATTN_DIM:]
    for hh in range(N_ATTN_HEADS):
        o_ref[:, hh * ATTN_DIM:(hh + 1) * ATTN_DIM] = out[hh * Q_BLOCK:(hh + 1) * Q_BLOCK].astype(BF16)


def _dsa_attention(p3, iw3, q_norm_g, k_norm_g, rel_bias):
    bsz, s, _ = p3.shape
    nq = s // Q_BLOCK
    qw = N_ATTN_HEADS * ATTN_DIM
    iqw = N_IDX_HEADS * IDX_DIM
    small0 = (qw + iqw) // LANES

    def small(off):
        return pl.BlockSpec((None, s, LANES), lambda b, i: (b, 0, small0 + off))

    return pl.pallas_call(
        functools.partial(_dsa_kernel, topk=min(MAX_TOPK, s // 4)),
        grid=(bsz, nq),
        in_specs=[pl.BlockSpec(memory_space=pltpu.SMEM),
                  pl.BlockSpec((None, Q_BLOCK, qw), lambda b, i: (b, i, 0)),
                  pl.BlockSpec((None, Q_BLOCK, iqw), lambda b, i: (b, i, qw // iqw)),
                  pl.BlockSpec((None, Q_BLOCK, LANES), lambda b, i: (b, i, 0)),
                  small(0), small(1), small(2), small(3),
                  pl.BlockSpec((1, ATTN_DIM), lambda b, i: (0, 0)),
                  pl.BlockSpec((1, ATTN_DIM), lambda b, i: (0, 0))],
        out_specs=pl.BlockSpec((None, Q_BLOCK, qw), lambda b, i: (b, i, 0)),
        out_shape=jax.ShapeDtypeStruct((bsz, s, qw), BF16),
        scratch_shapes=[pltpu.VMEM((s, 2 * ATTN_DIM), BF16),
                        pltpu.VMEM((nq, 2 * Q_BLOCK, LANES), BF16),
                        pltpu.VMEM((s, 2 * ATTN_DIM), BF16),
                        pltpu.VMEM((s // KEY_TILE, KEY_TILE, Q_BLOCK), I32),
                        pltpu.VMEM((3, N_ATTN_HEADS, Q_BLOCK, KEY_TILE), F32),
                        pltpu.VMEM((N_IDX_HEADS, Q_BLOCK, Q_BLOCK), F32),
                        pltpu.VMEM((N_ATTN_HEADS * Q_BLOCK, 2 * ATTN_DIM), BF16),
                        pltpu.VMEM((N_IDX_HEADS // 2 * Q_BLOCK, LANES), BF16),
                        pltpu.VMEM((N_ATTN_HEADS * Q_BLOCK, 2 * ATTN_DIM), F32),
                        pltpu.VMEM((N_ATTN_HEADS, Q_BLOCK, ATTN_DIM), F32)],
        compiler_params=_params("arbitrary", "arbitrary"),
        name="dsa_attention",
    )(rel_bias, p3, p3, iw3, p3, p3, p3, p3, q_norm_g.reshape(1, ATTN_DIM), k_norm_g.reshape(1, ATTN_DIM))


def _outproj_kernel(ya_ref, yb_ref, x_ref, w_ref, g1_ref, n2_ref, sc2_ref, sh2_ref, x1_ref, h2_ref):
    half = ya_ref.shape[1]
    mix = _dot(ya_ref[...], w_ref[:half, :]) + _dot(yb_ref[...], w_ref[half:, :])
    x1 = x_ref[...] + g1_ref[...] * mix
    x1_ref[...] = x1
    h2_ref[...] = _norm_modulate(x1, n2_ref[...], sc2_ref[...], sh2_ref[...]).astype(BF16)


def _out_projection(ya, yb, yb_col, x2d, w_out, norm2_g, mod3, layer, bsz, tm):
    m, d = x2d.shape
    tpb = (m // bsz) // tm
    row = pl.BlockSpec((tm, d), lambda i: (i, 0))
    return pl.pallas_call(
        _outproj_kernel,
        grid=(m // tm,),
        in_specs=[pl.BlockSpec((tm, d // 2), lambda i: (i, 0)),
                  pl.BlockSpec((tm, d // 2), lambda i: (i, yb_col)),
                  row,
                  pl.BlockSpec((d, d), lambda i: (0, 0)),
                  _mod_spec(layer, 2, bsz, tpb),
                  pl.BlockSpec((1, d), lambda i: (0, 0)),
                  _mod_spec(layer, 4, bsz, tpb),
                  _mod_spec(layer, 3, bsz, tpb)],
        out_specs=[row, row],
        out_shape=[jax.ShapeDtypeStruct((m, d), F32), jax.ShapeDtypeStruct((m, d), BF16)],
        compiler_params=_params("arbitrary"),
        name=f"out_projection_{layer}",
    )(ya, yb, x2d, w_out, mod3, norm2_g.reshape(1, d), mod3, mod3)


def _mlp_kernel(h_ref, x1_ref, w1_ref, w2_ref, g2_ref, o_ref):
    j = pl.program_id(1)

    @pl.when(j == 0)
    def _():
        o_ref[...] = jnp.zeros_like(o_ref)

    a = jnp.square(jnp.maximum(_dot(h_ref[...], w1_ref[...]), 0.0)).astype(BF16)
    o_ref[...] += _dot(a, w2_ref[...])

    @pl.when(j == pl.num_programs(1) - 1)
    def _():
        o_ref[...] = x1_ref[...] + g2_ref[...] * o_ref[...]


def _mlp(h2, x1, w1, w2, mod3, layer, bsz, tm, tf):
    m, d = x1.shape
    ff = w1.shape[-1]
    tpb = (m // bsz) // tm
    row = pl.BlockSpec((tm, d), lambda i, j: (i, 0))
    return pl.pallas_call(
        _mlp_kernel,
        grid=(m // tm, ff // tf),
        in_specs=[row, row,
                  pl.BlockSpec((None, d, tf), lambda i, j: (layer, 0, j)),
                  pl.BlockSpec((None, tf, d), lambda i, j: (layer, j, 0)),
                  _mod_spec(layer, 5, bsz, tpb)],
        out_specs=row,
        out_shape=jax.ShapeDtypeStruct((m, d), F32),
        compiler_params=_params("arbitrary", "arbitrary"),
        name=f"mlp_{layer}",
    )(h2, x1, w1, w2, mod3)


def _pad_cols(w, width):
    return jnp.pad(w, ((0, 0), (0, width - w.shape[1])))


def _even_weights(w_in):
    gw = GROUP_WIDTH
    g0 = 4 * gw
    g1 = g0 + 2 * N_GROUP_HEADS
    main = jnp.concatenate([w_in[:, :g0], w_in[:, g1:]], axis=1).astype(BF16)
    tail = _pad_cols(w_in[:, g0:g1], LANES).astype(BF16)
    return main, tail


def _odd_weights(w_in):
    qw = N_ATTN_HEADS * ATTN_DIM
    iqw = N_IDX_HEADS * IDX_DIM
    o_k, o_v, o_iq = qw, qw + ATTN_DIM, qw + 2 * ATTN_DIM
    o_ik = o_iq + iqw
    o_iw = o_ik + IDX_DIM
    w_ik = w_in[:, o_ik:o_iw]
    zeros = jnp.zeros_like(w_ik)
    main = jnp.concatenate([w_in[:, :qw], w_in[:, o_iq:o_ik], w_in[:, o_k:o_v], w_in[:, o_v:o_iq],
                            w_ik, zeros, zeros, w_ik], axis=1).astype(BF16)
    tail = _pad_cols(w_in[:, o_iw:], LANES).astype(BF16)
    return main, tail


def kernel(x, c, ada_w, ada_b, norm1_g, norm2_g, mlp_w1, mlp_w2, even_w_in, even_conv_w, even_gate_b,
           even_head_norm_g, even_w_out, odd_w_in, odd_q_norm_g, odd_k_norm_g, odd_w_out, rel_bias):
    bsz, s, d = x.shape
    depth = ada_w.shape[0]
    m = bsz * s
    tm = ROW_TILE
    mod3 = _ada_modulation(c, ada_w, ada_b).reshape(depth * bsz * 6, 1, d)
    cos, sin = _rope_tables(s)
    w1_all = mlp_w1.astype(BF16)
    w2_all = mlp_w2.astype(BF16)
    xc = x.reshape(m, d)
    for l in range(depth):
        e = l // 2
        if l % 2 == 0:
            w_main, w_tail = _even_weights(even_w_in[e])
            p, gates = _in_projection(xc, norm1_g[l], mod3, l, bsz, w_main, w_tail, PROJ_ROW_TILE, 1024)
            p3 = p.reshape(bsz, s, -1)
            gates_col = gates.reshape(bsz, s, LANES)
            gates_row = jnp.swapaxes(
                gates_col[:, :, :2 * N_GROUP_HEADS].reshape(bsz, s // CHUNK, CHUNK, 2 * N_GROUP_HEADS), 2, 3)
            ya = _mlstm_heads(p3, gates_col, gates_row, even_conv_w[e], even_gate_b[e], even_head_norm_g[e])
            yb = _retention_heads(p3, cos, sin, even_head_norm_g[e])
            ya, yb, yb_col = ya.reshape(m, d // 2), yb.reshape(m, d // 2), 0
            w_out = even_w_out[e]
        else:
            w_main, w_tail = _odd_weights(odd_w_in[e])
            p, iw = _in_projection(xc, norm1_g[l], mod3, l, bsz, w_main, w_tail, PROJ_ROW_TILE, 512)
            y = _dsa_attention(p.reshape(bsz, s, -1), iw.reshape(bsz, s, LANES),
                               odd_q_norm_g[e], odd_k_norm_g[e], rel_bias)
            ya = yb = y.reshape(m, d)
            yb_col = 1
            w_out = odd_w_out[e]
        x1, h2 = _out_projection(ya, yb, yb_col, xc, w_out.astype(BF16), norm2_g[l], mod3, l, bsz, tm)
        xc = _mlp(h2, x1, w1_all, w2_all, mod3, l, bsz, tm, FF_TILE)
    return xc.reshape(bsz, s, d)
```

```python
import functools
import math

import numpy as np
import jax
import jax.numpy as jnp
from jax import lax
from jax.experimental import pallas as pl
from jax.experimental.pallas import tpu as pltpu

F32 = jnp.float32
BF16 = jnp.bfloat16
I32 = jnp.int32

D_MODEL = 2048
D_FF = 4 * D_MODEL
EPS = 1e-6
CHUNK = 128
HEAD_DIM = 256
N_GROUP_HEADS = 4
MLSTM_HEADS_PER_STEP = 2
RET_HEADS_PER_STEP = 4
GROUP_WIDTH = N_GROUP_HEADS * HEAD_DIM
CONV_WIDTH = 4
GATE_SOFTCAP = 15.0
ROPE_BASE = 10000.0
N_ATTN_HEADS = 16
ATTN_DIM = 128
N_IDX_HEADS = 16
IDX_DIM = 64
MAX_TOPK = 256
Q_BLOCK = 128
KEY_TILE = 2 * Q_BLOCK
DSA_SUBS = 2
assert DSA_SUBS * Q_BLOCK == KEY_TILE
REL_BUCKETS = 32
REL_MAX_DISTANCE = 128
LANES = 128
COUNT_ROWS = 64
INT_MIN = -(2 ** 31)
NEG_BIG = -1e30
LOG2E = math.log2(math.e)
VMEM_LIMIT = 56 * 1024 * 1024
ROW_TILE = 512
PROJ_ROW_TILE = 1024
FF_TILE = 1024
EVEN_COL_TILE = 1024
ODD_COL_TILE = 896


def _t5_large_bucket_thresholds():
    max_exact = REL_BUCKETS // 2
    d = np.arange(0, 4 * REL_MAX_DISTANCE)
    large = max_exact + (np.log(np.maximum(d, 1) / max_exact) / math.log(REL_MAX_DISTANCE / max_exact)
                         * (REL_BUCKETS - max_exact)).astype(np.int64)
    bucket = np.where(d < max_exact, d, np.minimum(large, REL_BUCKETS - 1))
    return [int(d[bucket >= b].min()) for b in range(max_exact + 1, REL_BUCKETS)]


T5_THRESHOLDS = _t5_large_bucket_thresholds()
assert T5_THRESHOLDS[-1] <= Q_BLOCK + 1


def _params(*semantics):
    return pltpu.CompilerParams(dimension_semantics=semantics, vmem_limit_bytes=VMEM_LIMIT)


def _dot(a, b):
    return jnp.dot(a, b, preferred_element_type=F32)


def _dot_nt(a, b):
    return lax.dot_general(a, b, (((1,), (1,)), ((), ())), preferred_element_type=F32)


def _dot_tn(a, b):
    return lax.dot_general(a, b, (((0,), (0,)), ((), ())), preferred_element_type=F32)


def _sigmoid(x):
    return 1.0 / (1.0 + jnp.exp(-x))


def _log_sigmoid(x):
    return jnp.minimum(x, 0.0) - jnp.log1p(jnp.exp(-jnp.abs(x)))


def _softcap(x):
    return GATE_SOFTCAP * jnp.tanh(x / GATE_SOFTCAP)


def _norm_modulate(x, g, sc, sh):
    r = lax.rsqrt(jnp.mean(x * x, axis=-1, keepdims=True) + EPS)
    return (x * r * g) * (1.0 + sc) + sh


def _ada_kernel(c_ref, w_ref, b_ref, o_ref):
    c = c_ref[...]
    cond = c * _sigmoid(c)
    o_ref[...] = jnp.dot(cond, w_ref[...], preferred_element_type=F32,
                         precision=lax.Precision.HIGHEST) + b_ref[...]


def _ada_modulation(c, ada_w, ada_b):
    depth, d, n = ada_w.shape
    bsz = c.shape[0]
    tn = 1024
    return pl.pallas_call(
        _ada_kernel,
        grid=(depth, n // tn),
        in_specs=[pl.BlockSpec((bsz, d), lambda l, j: (0, 0)),
                  pl.BlockSpec((None, d, tn), lambda l, j: (l, 0, j)),
                  pl.BlockSpec((None, 1, tn), lambda l, j: (l, 0, j))],
        out_specs=pl.BlockSpec((None, bsz, tn), lambda l, j: (l, 0, j)),
        out_shape=jax.ShapeDtypeStruct((depth, bsz, n), F32),
        compiler_params=_params("arbitrary", "arbitrary"),
        name="ada_modulation",
    )(c, ada_w, ada_b.reshape(depth, 1, n))


def _mod_spec(layer, which, bsz, tiles_per_batch):
    return pl.BlockSpec((None, 1, D_MODEL),
                        lambda i, *_: ((layer * bsz + i // tiles_per_batch) * 6 + which, 0, 0))


def _proj_kernel(x_ref, g_ref, sc_ref, sh_ref, w_ref, wt_ref, p_ref, t_ref, h_s):
    @pl.when(pl.program_id(1) == 0)
    def _():
        h = _norm_modulate(x_ref[...], g_ref[...], sc_ref[...], sh_ref[...]).astype(BF16)
        h_s[...] = h
        t_ref[...] = _dot(h, wt_ref[...])

    p_ref[...] = _dot(h_s[...], w_ref[...]).astype(p_ref.dtype)


def _in_projection(x2d, norm_g, mod3, layer, bsz, w_main, w_tail, tm, tn):
    m, d = x2d.shape
    n = w_main.shape[1]
    tpb = (m // bsz) // tm
    return pl.pallas_call(
        _proj_kernel,
        grid=(m // tm, n // tn),
        in_specs=[pl.BlockSpec((tm, d), lambda i, j: (i, 0)),
                  pl.BlockSpec((1, d), lambda i, j: (0, 0)),
                  _mod_spec(layer, 1, bsz, tpb),
                  _mod_spec(layer, 0, bsz, tpb),
                  pl.BlockSpec((d, tn), lambda i, j: (0, j)),
                  pl.BlockSpec((d, LANES), lambda i, j: (0, 0))],
        out_specs=[pl.BlockSpec((tm, tn), lambda i, j: (i, j)),
                   pl.BlockSpec((tm, LANES), lambda i, j: (i, 0))],
        out_shape=[jax.ShapeDtypeStruct((m, n), BF16),
                   jax.ShapeDtypeStruct((m, LANES), F32)],
        scratch_shapes=[pltpu.VMEM((tm, d), BF16)],
        compiler_params=_params("arbitrary", "arbitrary"),
        name=f"in_projection_{layer}",
    )(x2d, norm_g.reshape(1, d), mod3, mod3, w_main, w_tail)


def _rope_kernel(cos_ref, sin_ref):
    s, half = cos_ref.shape
    pos = lax.broadcasted_iota(I32, (s, half), 0).astype(F32)
    idx = lax.broadcasted_iota(I32, (s, half), 1).astype(F32)
    inv = jnp.exp(idx * (-math.log(ROPE_BASE) / half))
    ang = pos * inv
    cos_ref[...] = jnp.cos(ang)
    sin_ref[...] = jnp.sin(ang)


def _rope_tables(s):
    half = HEAD_DIM // 2
    return pl.pallas_call(
        _rope_kernel,
        out_shape=[jax.ShapeDtypeStruct((s, half), F32)] * 2,
        name="rope_tables",
    )()


def _group_norm_gate(h, ng, gate):
    r = lax.rsqrt(jnp.mean(h * h, axis=-1, keepdims=True) + EPS)
    return (h * r * ng * gate).astype(BF16)


def _mlstm_kernel(gb_ref, q_ref, k_ref, v_ref, o_ref, cwq_ref, cwk_ref, gc_ref, gbrow_ref, gr_ref,
                  ng_ref, y_ref, q_s, k_s, c_s, n_s, m_s):
    hps = ng_ref.shape[0]
    h0 = pl.program_id(1) * hps
    s = q_ref.shape[0]
    rows = lax.broadcasted_iota(I32, (s, 1), 0)

    def conv_silu(x_ref, cw_ref):
        x = x_ref[...].astype(F32)
        acc = x * cw_ref[CONV_WIDTH - 1:CONV_WIDTH, :]
        for sft in range(1, CONV_WIDTH):
            xs = jnp.where(rows >= sft, pltpu.roll(x, sft, 0), 0.0)
            acc = acc + xs * cw_ref[CONV_WIDTH - 1 - sft:CONV_WIDTH - sft, :]
        return acc * _sigmoid(acc)

    q_s[...] = conv_silu(q_ref, cwq_ref).astype(BF16)
    k_s[...] = (conv_silu(k_ref, cwk_ref) * (HEAD_DIM ** -0.5)).astype(BF16)
    c_s[...] = jnp.zeros_like(c_s)
    n_s[...] = jnp.zeros_like(n_s)
    m_s[...] = jnp.zeros_like(m_s)

    lane = lax.broadcasted_iota(I32, (CHUNK, LANES), 1)
    ri = lax.broadcasted_iota(I32, (CHUNK, CHUNK), 0)
    ci = lax.broadcasted_iota(I32, (CHUNK, CHUNK), 1)
    tril = ri >= ci

    def head_chunk(c, st, gcap, j):
        h = h0 + j
        cols = slice(j * HEAD_DIM, (j + 1) * HEAD_DIM)
        li_col = jnp.sum(jnp.where(lane == h, gcap, 0.0), axis=1, keepdims=True)
        lf_col = jnp.sum(jnp.where(lane == h + N_GROUP_HEADS, _log_sigmoid(gcap), 0.0), axis=1, keepdims=True)
        li_row = _softcap(gr_ref[c, pl.ds(h, 1), :] + gb_ref[h])
        lf_row = _log_sigmoid(_softcap(gr_ref[c, pl.ds(h + N_GROUP_HEADS, 1), :] + gb_ref[h + N_GROUP_HEADS]))
        bcum_col = jnp.sum(jnp.where(tril, lf_row, 0.0), axis=1, keepdims=True)
        bcum_row = jnp.sum(jnp.where(ri <= ci, lf_col, 0.0), axis=0, keepdims=True)
        b_last = jnp.sum(lf_row, axis=1, keepdims=True)
        m_prev = m_s[j]

        dmat = jnp.where(tril, bcum_col - bcum_row + li_row, -jnp.inf)
        m_inter = bcum_col + m_prev
        m_row = jnp.maximum(jnp.max(dmat, axis=1, keepdims=True), m_inter)
        qc = q_s[pl.ds(st, CHUNK), cols]
        kc = k_s[pl.ds(st, CHUNK), cols]
        vc = v_ref[pl.ds(st, CHUNK), cols]
        sc = _dot_nt(qc, kc) * jnp.exp(dmat - m_row)
        inter = jnp.exp(m_inter - m_row)
        num = _dot(sc.astype(BF16), vc) + inter * _dot(qc, c_s[j].astype(BF16))
        den = (jnp.sum(sc, axis=1, keepdims=True)
               + inter * jnp.sum(qc.astype(F32) * n_s[j], axis=1, keepdims=True))
        h_out = num / jnp.maximum(jnp.abs(den), jnp.exp(-m_row))

        g_col = b_last - bcum_col + li_col
        m_new = jnp.maximum(b_last + m_prev, jnp.max(g_col, axis=0, keepdims=True))
        decay = jnp.exp(b_last + m_prev - m_new)
        kw = kc.astype(F32) * jnp.exp(g_col - m_new)
        c_s[j] = decay * c_s[j] + _dot_tn(kw.astype(BF16), vc)
        n_s[j] = decay * n_s[j] + jnp.sum(kw, axis=0, keepdims=True)
        m_s[j] = m_new

        gate = _sigmoid(o_ref[pl.ds(st, CHUNK), cols].astype(F32))
        y_ref[pl.ds(st, CHUNK), cols] = _group_norm_gate(h_out, ng_ref[j], gate)

    def body(c, carry):
        st = pl.multiple_of(c * CHUNK, CHUNK)
        gcap = _softcap(gc_ref[pl.ds(st, CHUNK), :] + gbrow_ref[...])
        for j in range(hps):
            head_chunk(c, st, gcap, j)
        return carry

    lax.fori_loop(0, s // CHUNK, body, 0)


def _mlstm_heads(p3, gates_col, gates_row, conv_w, gate_b, head_norm_g):
    bsz, s, _ = p3.shape
    nh = N_GROUP_HEADS
    hps = MLSTM_HEADS_PER_STEP
    nsteps = nh // hps
    w = hps * HEAD_DIM

    def col(group):
        return pl.BlockSpec((None, s, w), lambda b, h: (b, 0, group * nsteps + h))

    gb_row = jnp.zeros((1, LANES), F32).at[0, :2 * nh].set(gate_b)
    return pl.pallas_call(
        _mlstm_kernel,
        grid=(bsz, nsteps),
        in_specs=[pl.BlockSpec(memory_space=pltpu.SMEM),
                  col(0), col(1), col(2), col(3),
                  pl.BlockSpec((CONV_WIDTH, w), lambda b, h: (0, h)),
                  pl.BlockSpec((CONV_WIDTH, w), lambda b, h: (0, nsteps + h)),
                  pl.BlockSpec((None, s, LANES), lambda b, h: (b, 0, 0)),
                  pl.BlockSpec((1, LANES), lambda b, h: (0, 0)),
                  pl.BlockSpec((None, s // CHUNK, 2 * nh, CHUNK), lambda b, h: (b, 0, 0, 0)),
                  pl.BlockSpec((hps, 1, HEAD_DIM), lambda b, h: (h, 0, 0))],
        out_specs=pl.BlockSpec((None, s, w), lambda b, h: (b, 0, h)),
        out_shape=jax.ShapeDtypeStruct((bsz, s, GROUP_WIDTH), BF16),
        scratch_shapes=[pltpu.VMEM((s, w), BF16), pltpu.VMEM((s, w), BF16),
                        pltpu.VMEM((hps, HEAD_DIM, HEAD_DIM), F32), pltpu.VMEM((hps, 1, HEAD_DIM), F32),
                        pltpu.VMEM((hps, 1, 1), F32)],
        compiler_params=_params("arbitrary", "arbitrary"),
        name="mlstm_heads",
    )(gate_b, p3, p3, p3, p3, conv_w, conv_w, gates_col, gb_row, gates_row,
      head_norm_g.reshape(2 * nh, 1, HEAD_DIM))


def _ret_kernel(q_ref, k_ref, v_ref, g_ref, cos_ref, sin_ref, ng_ref, y_ref, q_s, k_s, r_s):
    hps = ng_ref.shape[0]
    h0 = pl.program_id(1) * hps
    s = q_ref.shape[0]
    half = HEAD_DIM // 2
    cos = cos_ref[...]
    sin = sin_ref[...]

    def rotate(x_ref, dst, scale):
        for j in range(hps):
            lo = slice(j * HEAD_DIM, j * HEAD_DIM + half)
            hi = slice(j * HEAD_DIM + half, (j + 1) * HEAD_DIM)
            x1 = x_ref[:, lo].astype(F32)
            x2 = x_ref[:, hi].astype(F32)
            dst[:, lo] = ((x1 * cos - x2 * sin) * scale).astype(BF16)
            dst[:, hi] = ((x2 * cos + x1 * sin) * scale).astype(BF16)

    rotate(q_ref, q_s, 1.0)
    rotate(k_ref, k_s, HEAD_DIM ** -0.5)
    r_s[...] = jnp.zeros_like(r_s)

    ri = lax.broadcasted_iota(I32, (CHUNK, CHUNK), 0)
    ci = lax.broadcasted_iota(I32, (CHUNK, CHUNK), 1)
    rel = (ri - ci).astype(F32)
    jcol = lax.broadcasted_iota(I32, (CHUNK, 1), 0).astype(F32)
    decays = []
    for j in range(hps):
        log_gamma = jnp.full((1, 1), math.log(1.0 - 2.0 ** -5.0), F32)
        for i in range(1, N_GROUP_HEADS):
            log_gamma = jnp.where(h0 + j == i, math.log(1.0 - 2.0 ** (-5.0 - i)), log_gamma)
        decays.append((jnp.where(rel >= 0, jnp.exp(jnp.maximum(rel, 0.0) * log_gamma), 0.0),
                       jnp.exp((jcol + 1.0) * log_gamma),
                       jnp.exp((CHUNK - 1.0 - jcol) * log_gamma),
                       jnp.exp(CHUNK * log_gamma)))

    def body(c, carry):
        st = pl.multiple_of(c * CHUNK, CHUNK)
        for j in range(hps):
            dmask, cross_decay, state_decay, chunk_decay = decays[j]
            cols = slice(j * HEAD_DIM, (j + 1) * HEAD_DIM)
            qc = q_s[pl.ds(st, CHUNK), cols]
            kc = k_s[pl.ds(st, CHUNK), cols]
            vc = v_ref[pl.ds(st, CHUNK), cols]
            inner = _dot((_dot_nt(qc, kc) * dmask).astype(BF16), vc)
            cross = _dot(qc, r_s[j].astype(BF16)) * cross_decay
            r_s[j] = chunk_decay * r_s[j] + _dot_tn((kc.astype(F32) * state_decay).astype(BF16), vc)
            g = g_ref[pl.ds(st, CHUNK), cols].astype(F32)
            y_ref[pl.ds(st, CHUNK), cols] = _group_norm_gate(inner + cross, ng_ref[j], g * _sigmoid(g))
        return carry

    lax.fori_loop(0, s // CHUNK, body, 0)


def _retention_heads(p3, cos, sin, head_norm_g):
    bsz, s, _ = p3.shape
    nh = N_GROUP_HEADS
    hps = RET_HEADS_PER_STEP
    nsteps = nh // hps
    w = hps * HEAD_DIM

    def col(group):
        return pl.BlockSpec((None, s, w), lambda b, h: (b, 0, group * nsteps + h))

    return pl.pallas_call(
        _ret_kernel,
        grid=(bsz, nsteps),
        in_specs=[col(4), col(5), col(6), col(7),
                  pl.BlockSpec((s, HEAD_DIM // 2), lambda b, h: (0, 0)),
                  pl.BlockSpec((s, HEAD_DIM // 2), lambda b, h: (0, 0)),
                  pl.BlockSpec((hps, 1, HEAD_DIM), lambda b, h: (nsteps + h, 0, 0))],
        out_specs=pl.BlockSpec((None, s, w), lambda b, h: (b, 0, h)),
        out_shape=jax.ShapeDtypeStruct((bsz, s, GROUP_WIDTH), BF16),
        scratch_shapes=[pltpu.VMEM((s, w), BF16), pltpu.VMEM((s, w), BF16),
                        pltpu.VMEM((hps, HEAD_DIM, HEAD_DIM), F32)],
        compiler_params=_params("arbitrary", "arbitrary"),
        name="retention_heads",
    )(p3, p3, p3, p3, cos, sin, head_norm_g.reshape(2 * nh, 1, HEAD_DIM))


def _order_key(x):
    bits = pltpu.bitcast(x, I32)
    return bits ^ ((bits >> 31) & 0x7FFFFFFF)


def _dsa_kernel(tab_ref, q_ref, iq_ref, iw_ref, k_ref, v_ref, ika_ref, ikb_ref, qg_ref, kg_ref, o_ref,
                kext_s, ikab_s, vaug_s, keyt_s, tb_s, wb_s, qall_s, iqall_s, acc_s, m_s, *, topk):
    b = pl.program_id(0)
    qq = pl.program_id(1)
    s = k_ref.shape[0]
    nkb = s // Q_BLOCK
    subs = range(DSA_SUBS)
    ri = lax.broadcasted_iota(I32, (Q_BLOCK, Q_BLOCK), 0)
    ci = lax.broadcasted_iota(I32, (Q_BLOCK, Q_BLOCK), 1)

    def qblock(sub):
        return DSA_SUBS * qq + sub

    @pl.when((b == 0) & (qq == 0))
    def _():
        buckets = []
        for back in range(2):
            dist = jnp.maximum(back * Q_BLOCK + ri - ci, 0)
            bucket = jnp.full(dist.shape, REL_BUCKETS // 2, I32)
            for thr in T5_THRESHOLDS:
                bucket = bucket + jnp.where(dist >= thr, 1, 0)
            buckets.append(jnp.where(dist < REL_BUCKETS // 2, dist, bucket))

        def head_body(hh, carry):
            far = tab_ref[REL_BUCKETS - 1, hh]
            tiles = []
            for bucket in buckets:
                tile = jnp.zeros(bucket.shape, F32)
                for bk in range(REL_BUCKETS):
                    tile = jnp.where(bucket == bk, (tab_ref[bk, hh] - far) * LOG2E, tile)
                tiles.append(tile)
            zeros = jnp.zeros((Q_BLOCK, Q_BLOCK), F32)
            for i, (first, second) in enumerate(((tiles[0], zeros), (tiles[1], tiles[0]), (zeros, tiles[1]))):
                tb_s[i, hh, :, :Q_BLOCK] = first
                tb_s[i, hh, :, Q_BLOCK:] = second
            return carry

        lax.fori_loop(0, N_ATTN_HEADS, head_body, 0)
        onehot = jnp.where(ri == ci, 1.0, 0.0).astype(BF16)
        for sub in subs:
            for hh in range(N_ATTN_HEADS):
                qall_s[sub, hh * Q_BLOCK:(hh + 1) * Q_BLOCK, ATTN_DIM:] = onehot

    @pl.when(qq == 0)
    def _():
        k = k_ref[...].astype(F32)
        r = lax.rsqrt(jnp.mean(k * k, axis=-1, keepdims=True) + EPS)
        kn = (k * r * kg_ref[...]).astype(BF16)
        for sub in subs:
            kext_s[sub, :, :ATTN_DIM] = kn
        for kt in range(nkb):
            ikab_s[kt, :Q_BLOCK, :] = ika_ref[kt * Q_BLOCK:(kt + 1) * Q_BLOCK, :]
            ikab_s[kt, Q_BLOCK:, :] = ikb_ref[kt * Q_BLOCK:(kt + 1) * Q_BLOCK, :]
        vaug_s[:, :ATTN_DIM] = v_ref[...]
        vaug_s[:, ATTN_DIM:] = jnp.ones((s, ATTN_DIM), BF16)

    ones = jnp.ones((ATTN_DIM, ATTN_DIM), BF16)
    pair_rows = N_IDX_HEADS // 2 * Q_BLOCK
    for sub in subs:
        rows = slice(sub * Q_BLOCK, (sub + 1) * Q_BLOCK)
        qf = jnp.concatenate([q_ref[rows, hh * ATTN_DIM:(hh + 1) * ATTN_DIM] for hh in range(N_ATTN_HEADS)],
                             axis=0).astype(F32)
        sq = qf * qf
        sq_hi = sq.astype(BF16)
        sq_lo = (sq - sq_hi.astype(F32)).astype(BF16)
        mean_sq = (_dot(sq_hi, ones) + _dot(sq_lo, ones)) * (1.0 / ATTN_DIM)
        qall_s[sub, :, :ATTN_DIM] = (qf * lax.rsqrt(mean_sq + EPS)
                                     * (qg_ref[...] * (ATTN_DIM ** -0.5 * LOG2E))).astype(BF16)
        for hp in range(N_IDX_HEADS // 2):
            iqall_s[sub * pair_rows + hp * Q_BLOCK:sub * pair_rows + (hp + 1) * Q_BLOCK, :] = (
                iq_ref[rows, hp * LANES:(hp + 1) * LANES])
        wv = iw_ref[rows, :] * ((N_IDX_HEADS ** -0.5) * (IDX_DIM ** -0.5))
        for hh in range(N_IDX_HEADS):
            wb_s[sub, hh] = jnp.broadcast_to(wv[:, hh:hh + 1], (Q_BLOCK, Q_BLOCK))

    n_tiles = qq + 1

    def score_body(kt, carry):
        for half in range(KEY_TILE // Q_BLOCK):
            kb = 2 * kt + half
            r = _dot_nt(iqall_s[...], ikab_s[kb])
            for sub in subs:
                acc = jnp.zeros((Q_BLOCK, Q_BLOCK), F32)
                for hp in range(N_IDX_HEADS // 2):
                    blk = r[sub * pair_rows + hp * Q_BLOCK:sub * pair_rows + (hp + 1) * Q_BLOCK]
                    acc = acc + jnp.maximum(blk[:, :Q_BLOCK], 0.0) * wb_s[sub, 2 * hp]
                    acc = acc + jnp.maximum(blk[:, Q_BLOCK:], 0.0) * wb_s[sub, 2 * hp + 1]
                causal_t = (kb * Q_BLOCK + ri) <= (qblock(sub) * Q_BLOCK + ci)
                keyt_s[sub, kt, half * Q_BLOCK:(half + 1) * Q_BLOCK, :] = jnp.where(
                    causal_t, _order_key(acc.T), INT_MIN)
        return carry

    lax.fori_loop(0, n_tiles, score_body, 0)

    def count(pred):
        def inner(kt, accs):
            out = []
            for sub in subs:
                hit = jnp.where(pred(keyt_s[sub, kt], sub), 1.0, 0.0)
                out.append(accs[sub] + jnp.sum(hit.reshape(KEY_TILE // COUNT_ROWS, COUNT_ROWS, Q_BLOCK), axis=0))
            return tuple(out)
        accs = lax.fori_loop(0, n_tiles, inner, tuple(jnp.zeros((COUNT_ROWS, Q_BLOCK), F32) for _ in subs))
        return tuple(jnp.sum(acc, axis=0, keepdims=True) for acc in accs)

    def count_ge(cands):
        return count(lambda keys, sub: keys >= cands[sub])

    zero = jnp.zeros((1, Q_BLOCK), I32)
    thr0 = tuple(jnp.where(n >= topk, zero, INT_MIN) for n in count_ge((zero,) * DSA_SUBS))

    def bit_body(i, thrs):
        cands = tuple(t + jnp.left_shift(jnp.int32(1), 30 - i) for t in thrs)
        counts = count_ge(cands)
        return tuple(jnp.where(counts[sub] >= topk, cands[sub], thrs[sub]) for sub in subs)

    thrs = lax.fori_loop(0, 31, bit_body, thr0)
    thrs = tuple(jnp.maximum(t, INT_MIN + 1) for t in thrs)

    tied_over = sum(jnp.where(n > topk, 1.0, 0.0) for n in count_ge(thrs))

    @pl.when(jnp.max(tied_over) > 0.0)
    def _():
        greater = count(lambda keys, sub: keys > thrs[sub])
        kr = lax.broadcasted_iota(I32, (KEY_TILE, KEY_TILE), 0)
        kc = lax.broadcasted_iota(I32, (KEY_TILE, KEY_TILE), 1)
        lower = jnp.where(kc <= kr, 1.0, 0.0).astype(BF16)

        def retire(kt, seen):
            out = []
            for sub in subs:
                keys = keyt_s[sub, kt]
                tied = keys == thrs[sub]
                prefix = _dot(lower, jnp.where(tied, 1.0, 0.0).astype(BF16)) + seen[sub]
                keyt_s[sub, kt] = jnp.where(tied & (prefix > topk - greater[sub]), INT_MIN, keys)
                out.append(prefix[KEY_TILE - 1:, :])
            return tuple(out)

        lax.fori_loop(0, n_tiles, retire, tuple(jnp.zeros((1, Q_BLOCK), F32) for _ in subs))

    acc_s[...] = jnp.zeros_like(acc_s)
    m_s[...] = jnp.full(m_s.shape, NEG_BIG, F32)

    def attn_tile(kt, biases):
        st = pl.multiple_of(kt * KEY_TILE, KEY_TILE)
        for sub in subs:
            kext_s[sub, pl.ds(st, KEY_TILE), ATTN_DIM:] = jnp.where(
                keyt_s[sub, kt] >= thrs[sub], 0.0, NEG_BIG).astype(BF16)
        for sub in subs:
            sc = _dot_nt(qall_s[sub], kext_s[sub, pl.ds(st, KEY_TILE), :])
            sc = sc.reshape(N_ATTN_HEADS, Q_BLOCK, KEY_TILE)
            if biases[sub] is not None:
                sc = sc + tb_s[biases[sub]]
            m_prev = m_s[sub]
            m_new = jnp.maximum(m_prev, jnp.max(sc, axis=-1, keepdims=True))
            alpha = jnp.exp2(m_prev - m_new).reshape(N_ATTN_HEADS * Q_BLOCK, ATTN_DIM)
            p = jnp.concatenate([jnp.exp2(sc[:, :, i * ATTN_DIM:(i + 1) * ATTN_DIM] - m_new)
                                 for i in range(KEY_TILE // ATTN_DIM)], axis=-1)
            p = p.reshape(N_ATTN_HEADS * Q_BLOCK, KEY_TILE)
            pv = _dot(p.astype(BF16), vaug_s[pl.ds(st, KEY_TILE), :])
            acc_s[sub, :, :ATTN_DIM] = alpha * acc_s[sub, :, :ATTN_DIM] + pv[:, :ATTN_DIM]
            acc_s[sub, :, ATTN_DIM:] = alpha * acc_s[sub, :, ATTN_DIM:] + pv[:, ATTN_DIM:]
            m_s[sub] = m_new

    def far_body(kt, carry):
        attn_tile(kt, (None,) * DSA_SUBS)
        return carry

    lax.fori_loop(0, qq - 1, far_body, 0)

    @pl.when(qq >= 1)
    def _():
        attn_tile(qq - 1, (2, None))

    attn_tile(qq, (0, 1))

    for sub in subs:
        out = acc_s[sub, :, :ATTN_DIM] / acc_s[sub, :, ATTN_DIM:]
        for hh in range(N_ATTN_HEADS):
            o_ref[sub * Q_BLOCK:(sub + 1) * Q_BLOCK, hh * ATTN_DIM:(hh + 1) * ATTN_DIM] = (
                out[hh * Q_BLOCK:(hh + 1) * Q_BLOCK].astype(BF16))


def _dsa_attention(p3, iw3, q_norm_g, k_norm_g, rel_bias):
    bsz, s, _ = p3.shape
    nq = s // Q_BLOCK
    qw = N_ATTN_HEADS * ATTN_DIM
    iqw = N_IDX_HEADS * IDX_DIM
    small0 = (qw + iqw) // LANES
    step_rows = DSA_SUBS * Q_BLOCK

    def small(off):
        return pl.BlockSpec((None, s, LANES), lambda b, i: (b, 0, small0 + off))

    return pl.pallas_call(
        functools.partial(_dsa_kernel, topk=min(MAX_TOPK, s // 4)),
        grid=(bsz, nq // DSA_SUBS),
        in_specs=[pl.BlockSpec(memory_space=pltpu.SMEM),
                  pl.BlockSpec((None, step_rows, qw), lambda b, i: (b, i, 0)),
                  pl.BlockSpec((None, step_rows, iqw), lambda b, i: (b, i, qw // iqw)),
                  pl.BlockSpec((None, step_rows, LANES), lambda b, i: (b, i, 0)),
                  small(0), small(1), small(2), small(3),
                  pl.BlockSpec((1, ATTN_DIM), lambda b, i: (0, 0)),
                  pl.BlockSpec((1, ATTN_DIM), lambda b, i: (0, 0))],
        out_specs=pl.BlockSpec((None, step_rows, qw), lambda b, i: (b, i, 0)),
        out_shape=jax.ShapeDtypeStruct((bsz, s, qw), BF16),
        scratch_shapes=[pltpu.VMEM((DSA_SUBS, s, 2 * ATTN_DIM), BF16),
                        pltpu.VMEM((nq, 2 * Q_BLOCK, LANES), BF16),
                        pltpu.VMEM((s, 2 * ATTN_DIM), BF16),
                        pltpu.VMEM((DSA_SUBS, s // KEY_TILE, KEY_TILE, Q_BLOCK), I32),
                        pltpu.VMEM((3, N_ATTN_HEADS, Q_BLOCK, KEY_TILE), F32),
                        pltpu.VMEM((DSA_SUBS, N_IDX_HEADS, Q_BLOCK, Q_BLOCK), F32),
                        pltpu.VMEM((DSA_SUBS, N_ATTN_HEADS * Q_BLOCK, 2 * ATTN_DIM), BF16),
                        pltpu.VMEM((DSA_SUBS * N_IDX_HEADS // 2 * Q_BLOCK, LANES), BF16),
                        pltpu.VMEM((DSA_SUBS, N_ATTN_HEADS * Q_BLOCK, 2 * ATTN_DIM), F32),
                        pltpu.VMEM((DSA_SUBS, N_ATTN_HEADS, Q_BLOCK, ATTN_DIM), F32)],
        compiler_params=_params("arbitrary", "arbitrary"),
        name="dsa_attention",
    )(rel_bias, p3, p3, iw3, p3, p3, p3, p3, q_norm_g.reshape(1, ATTN_DIM), k_norm_g.reshape(1, ATTN_DIM))


def _outproj_kernel(ya_ref, yb_ref, x_ref, w_ref, g1_ref, n2_ref, sc2_ref, sh2_ref, x1_ref, h2_ref):
    half = ya_ref.shape[1]
    mix = _dot(ya_ref[...], w_ref[:half, :]) + _dot(yb_ref[...], w_ref[half:, :])
    x1 = x_ref[...] + g1_ref[...] * mix
    x1_ref[...] = x1
    h2_ref[...] = _norm_modulate(x1, n2_ref[...], sc2_ref[...], sh2_ref[...]).astype(BF16)


def _out_projection(ya, yb, yb_col, x2d, w_out, norm2_g, mod3, layer, bsz, tm):
    m, d = x2d.shape
    tpb = (m // bsz) // tm
    row = pl.BlockSpec((tm, d), lambda i: (i, 0))
    return pl.pallas_call(
        _outproj_kernel,
        grid=(m // tm,),
        in_specs=[pl.BlockSpec((tm, d // 2), lambda i: (i, 0)),
                  pl.BlockSpec((tm, d // 2), lambda i: (i, yb_col)),
                  row,
                  pl.BlockSpec((d, d), lambda i: (0, 0)),
                  _mod_spec(layer, 2, bsz, tpb),
                  pl.BlockSpec((1, d), lambda i: (0, 0)),
                  _mod_spec(layer, 4, bsz, tpb),
                  _mod_spec(layer, 3, bsz, tpb)],
        out_specs=[row, row],
        out_shape=[jax.ShapeDtypeStruct((m, d), F32), jax.ShapeDtypeStruct((m, d), BF16)],
        compiler_params=_params("arbitrary"),
        name=f"out_projection_{layer}",
    )(ya, yb, x2d, w_out, mod3, norm2_g.reshape(1, d), mod3, mod3)


def _mlp_kernel(h_ref, x1_ref, w1_ref, w2_ref, g2_ref, o_ref):
    j = pl.program_id(1)

    @pl.when(j == 0)
    def _():
        o_ref[...] = jnp.zeros_like(o_ref)

    a = jnp.square(jnp.maximum(_dot(h_ref[...], w1_ref[...]), 0.0)).astype(BF16)
    o_ref[...] += _dot(a, w2_ref[...])

    @pl.when(j == pl.num_programs(1) - 1)
    def _():
        o_ref[...] = x1_ref[...] + g2_ref[...] * o_ref[...]


def _mlp(h2, x1, w1, w2, mod3, layer, bsz, tm, tf):
    m, d = x1.shape
    ff = w1.shape[-1]
    tpb = (m // bsz) // tm
    row = pl.BlockSpec((tm, d), lambda i, j: (i, 0))
    return pl.pallas_call(
        _mlp_kernel,
        grid=(m // tm, ff // tf),
        in_specs=[row, row,
                  pl.BlockSpec((None, d, tf), lambda i, j: (layer, 0, j)),
                  pl.BlockSpec((None, tf, d), lambda i, j: (layer, j, 0)),
                  _mod_spec(layer, 5, bsz, tpb)],
        out_specs=row,
        out_shape=jax.ShapeDtypeStruct((m, d), F32),
        compiler_params=_params("arbitrary", "arbitrary"),
        name=f"mlp_{layer}",
    )(h2, x1, w1, w2, mod3)


def _pad_cols(w, width):
    return jnp.pad(w, ((0, 0), (0, width - w.shape[1])))


def _even_weights(w_in):
    gw = GROUP_WIDTH
    g0 = 4 * gw
    g1 = g0 + 2 * N_GROUP_HEADS
    main = jnp.concatenate([w_in[:, :g0], w_in[:, g1:]], axis=1).astype(BF16)
    tail = _pad_cols(w_in[:, g0:g1], LANES).astype(BF16)
    return main, tail


def _odd_weights(w_in):
    qw = N_ATTN_HEADS * ATTN_DIM
    iqw = N_IDX_HEADS * IDX_DIM
    o_k, o_v, o_iq = qw, qw + ATTN_DIM, qw + 2 * ATTN_DIM
    o_ik = o_iq + iqw
    o_iw = o_ik + IDX_DIM
    w_ik = w_in[:, o_ik:o_iw]
    zeros = jnp.zeros_like(w_ik)
    main = jnp.concatenate([w_in[:, :qw], w_in[:, o_iq:o_ik], w_in[:, o_k:o_v], w_in[:, o_v:o_iq],
                            w_ik, zeros, zeros, w_ik], axis=1).astype(BF16)
    tail = _pad_cols(w_in[:, o_iw:], LANES).astype(BF16)
    return main, tail


def kernel(x, c, ada_w, ada_b, norm1_g, norm2_g, mlp_w1, mlp_w2, even_w_in, even_conv_w, even_gate_b,
           even_head_norm_g, even_w_out, odd_w_in, odd_q_norm_g, odd_k_norm_g, odd_w_out, rel_bias):
    bsz, s, d = x.shape
    depth = ada_w.shape[0]
    m = bsz * s
    tm = ROW_TILE
    mod3 = _ada_modulation(c, ada_w, ada_b).reshape(depth * bsz * 6, 1, d)
    cos, sin = _rope_tables(s)
    w1_all = mlp_w1.astype(BF16)
    w2_all = mlp_w2.astype(BF16)
    xc = x.reshape(m, d)
    for l in range(depth):
        e = l // 2
        if l % 2 == 0:
            w_main, w_tail = _even_weights(even_w_in[e])
            p, gates = _in_projection(xc, norm1_g[l], mod3, l, bsz, w_main, w_tail, PROJ_ROW_TILE, EVEN_COL_TILE)
            p3 = p.reshape(bsz, s, -1)
            gates_col = gates.reshape(bsz, s, LANES)
            gates_row = jnp.swapaxes(
                gates_col[:, :, :2 * N_GROUP_HEADS].reshape(bsz, s // CHUNK, CHUNK, 2 * N_GROUP_HEADS), 2, 3)
            ya = _mlstm_heads(p3, gates_col, gates_row, even_conv_w[e], even_gate_b[e], even_head_norm_g[e])
            yb = _retention_heads(p3, cos, sin, even_head_norm_g[e])
            ya, yb, yb_col = ya.reshape(m, d // 2), yb.reshape(m, d // 2), 0
            w_out = even_w_out[e]
        else:
            w_main, w_tail = _odd_weights(odd_w_in[e])
            p, iw = _in_projection(xc, norm1_g[l], mod3, l, bsz, w_main, w_tail, PROJ_ROW_TILE, ODD_COL_TILE)
            y = _dsa_attention(p.reshape(bsz, s, -1), iw.reshape(bsz, s, LANES),
                               odd_q_norm_g[e], odd_k_norm_g[e], rel_bias)
            ya = yb = y.reshape(m, d)
            yb_col = 1
            w_out = odd_w_out[e]
        x1, h2 = _out_projection(ya, yb, yb_col, xc, w_out.astype(BF16), norm2_g[l], mod3, l, bsz, tm)
        xc = _mlp(h2, x1, w1_all, w2_all, mod3, l, bsz, tm, FF_TILE)
    return xc.reshape(bsz, s, d)
```

```python
import functools
import math

import numpy as np
import jax
import jax.numpy as jnp
from jax import lax
from jax.experimental import pallas as pl
from jax.experimental.pallas import tpu as pltpu

F32 = jnp.float32
BF16 = jnp.bfloat16
I32 = jnp.int32
I16 = jnp.int16

D_MODEL = 2048
D_FF = 4 * D_MODEL
EPS = 1e-6
CHUNK = 128
HEAD_DIM = 256
N_GROUP_HEADS = 4
MLSTM_HEADS_PER_STEP = 2
RET_HEADS_PER_STEP = 4
GROUP_WIDTH = N_GROUP_HEADS * HEAD_DIM
CONV_WIDTH = 4
GATE_SOFTCAP = 15.0
ROPE_BASE = 10000.0
N_ATTN_HEADS = 16
ATTN_DIM = 128
N_IDX_HEADS = 16
IDX_DIM = 64
MAX_TOPK = 256
Q_BLOCK = 128
KEY_TILE = 2 * Q_BLOCK
DSA_SUBS = 2
assert DSA_SUBS * Q_BLOCK == KEY_TILE
REL_BUCKETS = 32
REL_MAX_DISTANCE = 128
LANES = 128
COUNT_ROWS = 64
INT_MIN = -(2 ** 31)
NEG_BIG = -1e30
LOG2E = math.log2(math.e)
VMEM_LIMIT = 56 * 1024 * 1024
ROW_TILE = 512
PROJ_ROW_TILE = 1024
OUTPROJ_ROW_CHUNKS = 2
FF_TILE = 1024
EVEN_COL_TILE = 2048
ODD_COL_TILE = 1792


def _t5_large_bucket_thresholds():
    max_exact = REL_BUCKETS // 2
    d = np.arange(0, 4 * REL_MAX_DISTANCE)
    large = max_exact + (np.log(np.maximum(d, 1) / max_exact) / math.log(REL_MAX_DISTANCE / max_exact)
                         * (REL_BUCKETS - max_exact)).astype(np.int64)
    bucket = np.where(d < max_exact, d, np.minimum(large, REL_BUCKETS - 1))
    return [int(d[bucket >= b].min()) for b in range(max_exact + 1, REL_BUCKETS)]


T5_THRESHOLDS = _t5_large_bucket_thresholds()
assert T5_THRESHOLDS[-1] <= Q_BLOCK + 1


def _params(*semantics):
    return pltpu.CompilerParams(dimension_semantics=semantics, vmem_limit_bytes=VMEM_LIMIT)


def _dot(a, b):
    return jnp.dot(a, b, preferred_element_type=F32)


def _dot_nt(a, b):
    return lax.dot_general(a, b, (((1,), (1,)), ((), ())), preferred_element_type=F32)


def _dot_tn(a, b):
    return lax.dot_general(a, b, (((0,), (0,)), ((), ())), preferred_element_type=F32)


def _sigmoid(x):
    return 1.0 / (1.0 + jnp.exp(-x))


def _log_sigmoid(x):
    return jnp.minimum(x, 0.0) - jnp.log1p(jnp.exp(-jnp.abs(x)))


def _softcap(x):
    return GATE_SOFTCAP * jnp.tanh(x / GATE_SOFTCAP)


def _norm_modulate(x, g, sc, sh):
    r = lax.rsqrt(jnp.mean(x * x, axis=-1, keepdims=True) + EPS)
    return (x * r * g) * (1.0 + sc) + sh


def _ada_kernel(c_ref, w_ref, b_ref, o_ref):
    c = c_ref[...]
    cond = c * _sigmoid(c)
    o_ref[...] = jnp.dot(cond, w_ref[...], preferred_element_type=F32,
                         precision=lax.Precision.HIGHEST) + b_ref[...]


def _ada_modulation(c, ada_w, ada_b):
    depth, d, n = ada_w.shape
    bsz = c.shape[0]
    tn = 1024
    return pl.pallas_call(
        _ada_kernel,
        grid=(depth, n // tn),
        in_specs=[pl.BlockSpec((bsz, d), lambda l, j: (0, 0)),
                  pl.BlockSpec((None, d, tn), lambda l, j: (l, 0, j)),
                  pl.BlockSpec((None, 1, tn), lambda l, j: (l, 0, j))],
        out_specs=pl.BlockSpec((None, bsz, tn), lambda l, j: (l, 0, j)),
        out_shape=jax.ShapeDtypeStruct((depth, bsz, n), F32),
        compiler_params=_params("arbitrary", "arbitrary"),
        name="ada_modulation",
    )(c, ada_w, ada_b.reshape(depth, 1, n))


def _mod_spec(layer, which, bsz, tiles_per_batch):
    return pl.BlockSpec((None, 1, D_MODEL),
                        lambda i, *_: ((layer * bsz + i // tiles_per_batch) * 6 + which, 0, 0))


def _proj_kernel(x_ref, g_ref, sc_ref, sh_ref, w_ref, wt_ref, p_ref, t_ref, h_s):
    @pl.when(pl.program_id(1) == 0)
    def _():
        h = _norm_modulate(x_ref[...], g_ref[...], sc_ref[...], sh_ref[...]).astype(BF16)
        h_s[...] = h
        t_ref[...] = _dot(h, wt_ref[...])

    p_ref[...] = _dot(h_s[...], w_ref[...]).astype(p_ref.dtype)


def _in_projection(x2d, norm_g, mod3, layer, bsz, w_main, w_tail, tm, tn):
    m, d = x2d.shape
    n = w_main.shape[1]
    tpb = (m // bsz) // tm
    return pl.pallas_call(
        _proj_kernel,
        grid=(m // tm, n // tn),
        in_specs=[pl.BlockSpec((tm, d), lambda i, j: (i, 0)),
                  pl.BlockSpec((1, d), lambda i, j: (0, 0)),
                  _mod_spec(layer, 1, bsz, tpb),
                  _mod_spec(layer, 0, bsz, tpb),
                  pl.BlockSpec((d, tn), lambda i, j: (0, j)),
                  pl.BlockSpec((d, LANES), lambda i, j: (0, 0))],
        out_specs=[pl.BlockSpec((tm, tn), lambda i, j: (i, j)),
                   pl.BlockSpec((tm, LANES), lambda i, j: (i, 0))],
        out_shape=[jax.ShapeDtypeStruct((m, n), BF16),
                   jax.ShapeDtypeStruct((m, LANES), F32)],
        scratch_shapes=[pltpu.VMEM((tm, d), BF16)],
        compiler_params=_params("arbitrary", "arbitrary"),
        name=f"in_projection_{layer}",
    )(x2d, norm_g.reshape(1, d), mod3, mod3, w_main, w_tail)


def _rope_kernel(cos_ref, sin_ref):
    s, half = cos_ref.shape
    pos = lax.broadcasted_iota(I32, (s, half), 0).astype(F32)
    idx = lax.broadcasted_iota(I32, (s, half), 1).astype(F32)
    inv = jnp.exp(idx * (-math.log(ROPE_BASE) / half))
    ang = pos * inv
    cos_ref[...] = jnp.cos(ang)
    sin_ref[...] = jnp.sin(ang)


def _rope_tables(s):
    half = HEAD_DIM // 2
    return pl.pallas_call(
        _rope_kernel,
        out_shape=[jax.ShapeDtypeStruct((s, half), F32)] * 2,
        name="rope_tables",
    )()


def _group_norm_gate(h, ng, gate):
    r = lax.rsqrt(jnp.mean(h * h, axis=-1, keepdims=True) + EPS)
    return (h * r * ng * gate).astype(BF16)


def _mlstm_kernel(gb_ref, q_ref, k_ref, v_ref, o_ref, cwq_ref, cwk_ref, gc_ref, gbrow_ref, gr_ref,
                  ng_ref, y_ref, q_s, k_s, c_s, n_s, m_s):
    hps = ng_ref.shape[0]
    h0 = pl.program_id(1) * hps
    s = q_ref.shape[0]
    rows = lax.broadcasted_iota(I32, (s, 1), 0)

    def conv_silu(x_ref, cw_ref):
        x = x_ref[...].astype(F32)
        acc = x * cw_ref[CONV_WIDTH - 1:CONV_WIDTH, :]
        for sft in range(1, CONV_WIDTH):
            xs = jnp.where(rows >= sft, pltpu.roll(x, sft, 0), 0.0)
            acc = acc + xs * cw_ref[CONV_WIDTH - 1 - sft:CONV_WIDTH - sft, :]
        return acc * _sigmoid(acc)

    q_s[...] = conv_silu(q_ref, cwq_ref).astype(BF16)
    k_s[...] = (conv_silu(k_ref, cwk_ref) * (HEAD_DIM ** -0.5)).astype(BF16)
    c_s[...] = jnp.zeros_like(c_s)
    n_s[...] = jnp.zeros_like(n_s)
    m_s[...] = jnp.zeros_like(m_s)

    lane = lax.broadcasted_iota(I32, (CHUNK, LANES), 1)
    ri = lax.broadcasted_iota(I32, (CHUNK, CHUNK), 0)
    ci = lax.broadcasted_iota(I32, (CHUNK, CHUNK), 1)
    tril = ri >= ci

    def head_chunk(c, st, gcap, j):
        h = h0 + j
        cols = slice(j * HEAD_DIM, (j + 1) * HEAD_DIM)
        li_col = jnp.sum(jnp.where(lane == h, gcap, 0.0), axis=1, keepdims=True)
        lf_col = jnp.sum(jnp.where(lane == h + N_GROUP_HEADS, _log_sigmoid(gcap), 0.0), axis=1, keepdims=True)
        li_row = _softcap(gr_ref[c, pl.ds(h, 1), :] + gb_ref[h])
        lf_row = _log_sigmoid(_softcap(gr_ref[c, pl.ds(h + N_GROUP_HEADS, 1), :] + gb_ref[h + N_GROUP_HEADS]))
        bcum_col = jnp.sum(jnp.where(tril, lf_row, 0.0), axis=1, keepdims=True)
        bcum_row = jnp.sum(jnp.where(ri <= ci, lf_col, 0.0), axis=0, keepdims=True)
        b_last = jnp.sum(lf_row, axis=1, keepdims=True)
        m_prev = m_s[j]

        dmat = jnp.where(tril, bcum_col - bcum_row + li_row, -jnp.inf)
        m_inter = bcum_col + m_prev
        m_row = jnp.maximum(jnp.max(dmat, axis=1, keepdims=True), m_inter)
        qc = q_s[pl.ds(st, CHUNK), cols]
        kc = k_s[pl.ds(st, CHUNK), cols]
        vc = v_ref[pl.ds(st, CHUNK), cols]
        sc = _dot_nt(qc, kc) * jnp.exp(dmat - m_row)
        inter = jnp.exp(m_inter - m_row)
        num = _dot(sc.astype(BF16), vc) + inter * _dot(qc, c_s[j].astype(BF16))
        den = (jnp.sum(sc, axis=1, keepdims=True)
               + inter * jnp.sum(qc.astype(F32) * n_s[j], axis=1, keepdims=True))
        h_out = num / jnp.maximum(jnp.abs(den), jnp.exp(-m_row))

        g_col = b_last - bcum_col + li_col
        m_new = jnp.maximum(b_last + m_prev, jnp.max(g_col, axis=0, keepdims=True))
        decay = jnp.exp(b_last + m_prev - m_new)
        kw = kc.astype(F32) * jnp.exp(g_col - m_new)
        c_s[j] = decay * c_s[j] + _dot_tn(kw.astype(BF16), vc)
        n_s[j] = decay * n_s[j] + jnp.sum(kw, axis=0, keepdims=True)
        m_s[j] = m_new

        gate = _sigmoid(o_ref[pl.ds(st, CHUNK), cols].astype(F32))
        y_ref[pl.ds(st, CHUNK), cols] = _group_norm_gate(h_out, ng_ref[j], gate)

    def body(c, carry):
        st = pl.multiple_of(c * CHUNK, CHUNK)
        gcap = _softcap(gc_ref[pl.ds(st, CHUNK), :] + gbrow_ref[...])
        for j in range(hps):
            head_chunk(c, st, gcap, j)
        return carry

    lax.fori_loop(0, s // CHUNK, body, 0)


def _mlstm_heads(p3, gates_col, gates_row, conv_w, gate_b, head_norm_g):
    bsz, s, _ = p3.shape
    nh = N_GROUP_HEADS
    hps = MLSTM_HEADS_PER_STEP
    nsteps = nh // hps
    w = hps * HEAD_DIM

    def col(group):
        return pl.BlockSpec((None, s, w), lambda b, h: (b, 0, group * nsteps + h))

    gb_row = jnp.zeros((1, LANES), F32).at[0, :2 * nh].set(gate_b)
    return pl.pallas_call(
        _mlstm_kernel,
        grid=(bsz, nsteps),
        in_specs=[pl.BlockSpec(memory_space=pltpu.SMEM),
                  col(0), col(1), col(2), col(3),
                  pl.BlockSpec((CONV_WIDTH, w), lambda b, h: (0, h)),
                  pl.BlockSpec((CONV_WIDTH, w), lambda b, h: (0, nsteps + h)),
                  pl.BlockSpec((None, s, LANES), lambda b, h: (b, 0, 0)),
                  pl.BlockSpec((1, LANES), lambda b, h: (0, 0)),
                  pl.BlockSpec((None, s // CHUNK, 2 * nh, CHUNK), lambda b, h: (b, 0, 0, 0)),
                  pl.BlockSpec((hps, 1, HEAD_DIM), lambda b, h: (h, 0, 0))],
        out_specs=pl.BlockSpec((None, s, w), lambda b, h: (b, 0, h)),
        out_shape=jax.ShapeDtypeStruct((bsz, s, GROUP_WIDTH), BF16),
        scratch_shapes=[pltpu.VMEM((s, w), BF16), pltpu.VMEM((s, w), BF16),
                        pltpu.VMEM((hps, HEAD_DIM, HEAD_DIM), F32), pltpu.VMEM((hps, 1, HEAD_DIM), F32),
                        pltpu.VMEM((hps, 1, 1), F32)],
        compiler_params=_params("arbitrary", "arbitrary"),
        name="mlstm_heads",
    )(gate_b, p3, p3, p3, p3, conv_w, conv_w, gates_col, gb_row, gates_row,
      head_norm_g.reshape(2 * nh, 1, HEAD_DIM))


def _ret_kernel(q_ref, k_ref, v_ref, g_ref, cos_ref, sin_ref, ng_ref, y_ref, q_s, k_s, r_s):
    hps = ng_ref.shape[0]
    h0 = pl.program_id(1) * hps
    s = q_ref.shape[0]
    half = HEAD_DIM // 2
    cos = cos_ref[...]
    sin = sin_ref[...]

    def rotate(x_ref, dst, scale):
        for j in range(hps):
            lo = slice(j * HEAD_DIM, j * HEAD_DIM + half)
            hi = slice(j * HEAD_DIM + half, (j + 1) * HEAD_DIM)
            x1 = x_ref[:, lo].astype(F32)
            x2 = x_ref[:, hi].astype(F32)
            dst[:, lo] = ((x1 * cos - x2 * sin) * scale).astype(BF16)
            dst[:, hi] = ((x2 * cos + x1 * sin) * scale).astype(BF16)

    rotate(q_ref, q_s, 1.0)
    rotate(k_ref, k_s, HEAD_DIM ** -0.5)
    r_s[...] = jnp.zeros_like(r_s)

    ri = lax.broadcasted_iota(I32, (CHUNK, CHUNK), 0)
    ci = lax.broadcasted_iota(I32, (CHUNK, CHUNK), 1)
    rel = (ri - ci).astype(F32)
    jcol = lax.broadcasted_iota(I32, (CHUNK, 1), 0).astype(F32)
    decays = []
    for j in range(hps):
        log_gamma = jnp.full((1, 1), math.log(1.0 - 2.0 ** -5.0), F32)
        for i in range(1, N_GROUP_HEADS):
            log_gamma = jnp.where(h0 + j == i, math.log(1.0 - 2.0 ** (-5.0 - i)), log_gamma)
        decays.append((jnp.where(rel >= 0, jnp.exp(jnp.maximum(rel, 0.0) * log_gamma), 0.0),
                       jnp.exp((jcol + 1.0) * log_gamma),
                       jnp.exp((CHUNK - 1.0 - jcol) * log_gamma),
                       jnp.exp(CHUNK * log_gamma)))

    def body(c, carry):
        st = pl.multiple_of(c * CHUNK, CHUNK)
        for j in range(hps):
            dmask, cross_decay, state_decay, chunk_decay = decays[j]
            cols = slice(j * HEAD_DIM, (j + 1) * HEAD_DIM)
            qc = q_s[pl.ds(st, CHUNK), cols]
            kc = k_s[pl.ds(st, CHUNK), cols]
            vc = v_ref[pl.ds(st, CHUNK), cols]
            inner = _dot((_dot_nt(qc, kc) * dmask).astype(BF16), vc)
            cross = _dot(qc, r_s[j].astype(BF16)) * cross_decay
            r_s[j] = chunk_decay * r_s[j] + _dot_tn((kc.astype(F32) * state_decay).astype(BF16), vc)
            g = g_ref[pl.ds(st, CHUNK), cols].astype(F32)
            y_ref[pl.ds(st, CHUNK), cols] = _group_norm_gate(inner + cross, ng_ref[j], g * _sigmoid(g))
        return carry

    lax.fori_loop(0, s // CHUNK, body, 0)


def _retention_heads(p3, cos, sin, head_norm_g):
    bsz, s, _ = p3.shape
    nh = N_GROUP_HEADS
    hps = RET_HEADS_PER_STEP
    nsteps = nh // hps
    w = hps * HEAD_DIM

    def col(group):
        return pl.BlockSpec((None, s, w), lambda b, h: (b, 0, group * nsteps + h))

    return pl.pallas_call(
        _ret_kernel,
        grid=(bsz, nsteps),
        in_specs=[col(4), col(5), col(6), col(7),
                  pl.BlockSpec((s, HEAD_DIM // 2), lambda b, h: (0, 0)),
                  pl.BlockSpec((s, HEAD_DIM // 2), lambda b, h: (0, 0)),
                  pl.BlockSpec((hps, 1, HEAD_DIM), lambda b, h: (nsteps + h, 0, 0))],
        out_specs=pl.BlockSpec((None, s, w), lambda b, h: (b, 0, h)),
        out_shape=jax.ShapeDtypeStruct((bsz, s, GROUP_WIDTH), BF16),
        scratch_shapes=[pltpu.VMEM((s, w), BF16), pltpu.VMEM((s, w), BF16),
                        pltpu.VMEM((hps, HEAD_DIM, HEAD_DIM), F32)],
        compiler_params=_params("arbitrary", "arbitrary"),
        name="retention_heads",
    )(p3, p3, p3, p3, cos, sin, head_norm_g.reshape(2 * nh, 1, HEAD_DIM))


def _order_key(x):
    bits = pltpu.bitcast(x, I32)
    return bits ^ ((bits >> 31) & 0x7FFFFFFF)


def _dsa_kernel(tab_ref, q_ref, iq_ref, iw_ref, k_ref, v_ref, ika_ref, ikb_ref, qg_ref, kg_ref, o_ref,
                kext_s, ikab_s, vaug_s, keyt_s, keyh_s, tb_s, wb_s, qall_s, iqall_s, acc_s, m_s, *, topk):
    b = pl.program_id(0)
    qq = pl.program_id(1)
    s = k_ref.shape[0]
    nkb = s // Q_BLOCK
    subs = range(DSA_SUBS)
    ri = lax.broadcasted_iota(I32, (Q_BLOCK, Q_BLOCK), 0)
    ci = lax.broadcasted_iota(I32, (Q_BLOCK, Q_BLOCK), 1)

    def qblock(sub):
        return DSA_SUBS * qq + sub

    @pl.when((b == 0) & (qq == 0))
    def _():
        buckets = []
        for back in range(2):
            dist = jnp.maximum(back * Q_BLOCK + ri - ci, 0)
            bucket = jnp.full(dist.shape, REL_BUCKETS // 2, I32)
            for thr in T5_THRESHOLDS:
                bucket = bucket + jnp.where(dist >= thr, 1, 0)
            buckets.append(jnp.where(dist < REL_BUCKETS // 2, dist, bucket))

        def head_body(hh, carry):
            far = tab_ref[REL_BUCKETS - 1, hh]
            tiles = []
            for bucket in buckets:
                tile = jnp.zeros(bucket.shape, F32)
                for bk in range(REL_BUCKETS):
                    tile = jnp.where(bucket == bk, (tab_ref[bk, hh] - far) * LOG2E, tile)
                tiles.append(tile)
            zeros = jnp.zeros((Q_BLOCK, Q_BLOCK), F32)
            for i, (first, second) in enumerate(((tiles[0], zeros), (tiles[1], tiles[0]), (zeros, tiles[1]))):
                tb_s[i, hh, :, :Q_BLOCK] = first
                tb_s[i, hh, :, Q_BLOCK:] = second
            return carry

        lax.fori_loop(0, N_ATTN_HEADS, head_body, 0)
        onehot = jnp.where(ri == ci, 1.0, 0.0).astype(BF16)
        for sub in subs:
            for hh in range(N_ATTN_HEADS):
                qall_s[sub, hh * Q_BLOCK:(hh + 1) * Q_BLOCK, ATTN_DIM:] = onehot

    @pl.when(qq == 0)
    def _():
        k = k_ref[...].astype(F32)
        r = lax.rsqrt(jnp.mean(k * k, axis=-1, keepdims=True) + EPS)
        kn = (k * r * kg_ref[...]).astype(BF16)
        for sub in subs:
            kext_s[sub, :, :ATTN_DIM] = kn
        for kt in range(nkb):
            ikab_s[kt, :Q_BLOCK, :] = ika_ref[kt * Q_BLOCK:(kt + 1) * Q_BLOCK, :]
            ikab_s[kt, Q_BLOCK:, :] = ikb_ref[kt * Q_BLOCK:(kt + 1) * Q_BLOCK, :]
        vaug_s[:, :ATTN_DIM] = v_ref[...]
        vaug_s[:, ATTN_DIM:] = jnp.ones((s, ATTN_DIM), BF16)

    ones = jnp.ones((ATTN_DIM, ATTN_DIM), BF16)
    pair_rows = N_IDX_HEADS // 2 * Q_BLOCK
    for sub in subs:
        rows = slice(sub * Q_BLOCK, (sub + 1) * Q_BLOCK)
        qf = jnp.concatenate([q_ref[rows, hh * ATTN_DIM:(hh + 1) * ATTN_DIM] for hh in range(N_ATTN_HEADS)],
                             axis=0).astype(F32)
        sq = qf * qf
        sq_hi = sq.astype(BF16)
        sq_lo = (sq - sq_hi.astype(F32)).astype(BF16)
        mean_sq = (_dot(sq_hi, ones) + _dot(sq_lo, ones)) * (1.0 / ATTN_DIM)
        qall_s[sub, :, :ATTN_DIM] = (qf * lax.rsqrt(mean_sq + EPS)
                                     * (qg_ref[...] * (ATTN_DIM ** -0.5 * LOG2E))).astype(BF16)
        for hp in range(N_IDX_HEADS // 2):
            iqall_s[sub * pair_rows + hp * Q_BLOCK:sub * pair_rows + (hp + 1) * Q_BLOCK, :] = (
                iq_ref[rows, hp * LANES:(hp + 1) * LANES])
        wv = iw_ref[rows, :] * ((N_IDX_HEADS ** -0.5) * (IDX_DIM ** -0.5))
        for hh in range(N_IDX_HEADS):
            wb_s[sub, hh] = jnp.broadcast_to(wv[:, hh:hh + 1], (Q_BLOCK, Q_BLOCK))

    n_tiles = qq + 1

    def score_body(kt, carry):
        for half in range(KEY_TILE // Q_BLOCK):
            kb = 2 * kt + half
            r = _dot_nt(iqall_s[...], ikab_s[kb])
            for sub in subs:
                acc = jnp.zeros((Q_BLOCK, Q_BLOCK), F32)
                for hp in range(N_IDX_HEADS // 2):
                    blk = r[sub * pair_rows + hp * Q_BLOCK:sub * pair_rows + (hp + 1) * Q_BLOCK]
                    acc = acc + jnp.maximum(blk[:, :Q_BLOCK], 0.0) * wb_s[sub, 2 * hp]
                    acc = acc + jnp.maximum(blk[:, Q_BLOCK:], 0.0) * wb_s[sub, 2 * hp + 1]
                causal_t = (kb * Q_BLOCK + ri) <= (qblock(sub) * Q_BLOCK + ci)
                key = jnp.where(causal_t, _order_key(acc.T), INT_MIN)
                keyt_s[sub, kt, half * Q_BLOCK:(half + 1) * Q_BLOCK, :] = key
                keyh_s[sub, kt, half * Q_BLOCK:(half + 1) * Q_BLOCK, :] = (key >> 16).astype(I16)
        return carry

    lax.fori_loop(0, n_tiles, score_body, 0)

    def count(pred):
        def inner(kt, accs):
            out = []
            for sub in subs:
                hit = jnp.where(pred(keyt_s[sub, kt], sub), 1.0, 0.0)
                out.append(accs[sub] + jnp.sum(hit.reshape(KEY_TILE // COUNT_ROWS, COUNT_ROWS, Q_BLOCK), axis=0))
            return tuple(out)
        accs = lax.fori_loop(0, n_tiles, inner, tuple(jnp.zeros((COUNT_ROWS, Q_BLOCK), F32) for _ in subs))
        return tuple(jnp.sum(acc, axis=0, keepdims=True) for acc in accs)

    def count_ge(cands):
        return count(lambda keys, sub: keys >= cands[sub])

    def count_hi_ge(cands):
        def inner(kt, accs):
            out = []
            for sub in subs:
                hit = jnp.where(keyh_s[sub, kt] >= cands[sub].astype(I16), jnp.int16(1), jnp.int16(0))
                acc = accs[sub]
                for part in range(KEY_TILE // COUNT_ROWS):
                    acc = acc + hit[part * COUNT_ROWS:(part + 1) * COUNT_ROWS]
                out.append(acc)
            return tuple(out)
        accs = lax.fori_loop(0, n_tiles, inner, tuple(jnp.zeros((COUNT_ROWS, Q_BLOCK), I16) for _ in subs))
        return tuple(jnp.sum(acc.astype(F32), axis=0, keepdims=True) for acc in accs)

    zero = jnp.zeros((1, Q_BLOCK), I32)
    hi0 = tuple(jnp.where(n >= topk, zero, -(2 ** 15)) for n in count_hi_ge((zero,) * DSA_SUBS))

    def hi_bit_body(i, his):
        cands = tuple(t + jnp.left_shift(jnp.int32(1), 14 - i) for t in his)
        counts = count_hi_ge(cands)
        return tuple(jnp.where(counts[sub] >= topk, cands[sub], his[sub]) for sub in subs)

    his = lax.fori_loop(0, 15, hi_bit_body, hi0)

    def lo_bit_body(i, thrs):
        cands = tuple(t + jnp.left_shift(jnp.int32(1), 15 - i) for t in thrs)
        counts = count_ge(cands)
        return tuple(jnp.where(counts[sub] >= topk, cands[sub], thrs[sub]) for sub in subs)

    thrs = lax.fori_loop(0, 16, lo_bit_body, tuple(t * (2 ** 16) for t in his))
    thrs = tuple(jnp.maximum(t, INT_MIN + 1) for t in thrs)

    tied_over = sum(jnp.where(n > topk, 1.0, 0.0) for n in count_ge(thrs))

    @pl.when(jnp.max(tied_over) > 0.0)
    def _():
        greater = count(lambda keys, sub: keys > thrs[sub])
        kr = lax.broadcasted_iota(I32, (KEY_TILE, KEY_TILE), 0)
        kc = lax.broadcasted_iota(I32, (KEY_TILE, KEY_TILE), 1)
        lower = jnp.where(kc <= kr, 1.0, 0.0).astype(BF16)

        def retire(kt, seen):
            out = []
            for sub in subs:
                keys = keyt_s[sub, kt]
                tied = keys == thrs[sub]
                prefix = _dot(lower, jnp.where(tied, 1.0, 0.0).astype(BF16)) + seen[sub]
                keyt_s[sub, kt] = jnp.where(tied & (prefix > topk - greater[sub]), INT_MIN, keys)
                out.append(prefix[KEY_TILE - 1:, :])
            return tuple(out)

        lax.fori_loop(0, n_tiles, retire, tuple(jnp.zeros((1, Q_BLOCK), F32) for _ in subs))

    acc_s[...] = jnp.zeros_like(acc_s)
    m_s[...] = jnp.full(m_s.shape, NEG_BIG, F32)

    def attn_tile(kt, biases):
        st = pl.multiple_of(kt * KEY_TILE, KEY_TILE)
        for sub in subs:
            kext_s[sub, pl.ds(st, KEY_TILE), ATTN_DIM:] = jnp.where(
                keyt_s[sub, kt] >= thrs[sub], 0.0, NEG_BIG).astype(BF16)
        for sub in subs:
            sc = _dot_nt(qall_s[sub], kext_s[sub, pl.ds(st, KEY_TILE), :])
            sc = sc.reshape(N_ATTN_HEADS, Q_BLOCK, KEY_TILE)
            if biases[sub] is not None:
                sc = sc + tb_s[biases[sub]]
            m_prev = m_s[sub]
            m_new = jnp.maximum(m_prev, jnp.max(sc, axis=-1, keepdims=True))
            alpha = jnp.exp2(m_prev - m_new).reshape(N_ATTN_HEADS * Q_BLOCK, ATTN_DIM)
            p = jnp.concatenate([jnp.exp2(sc[:, :, i * ATTN_DIM:(i + 1) * ATTN_DIM] - m_new)
                                 for i in range(KEY_TILE // ATTN_DIM)], axis=-1)
            p = p.reshape(N_ATTN_HEADS * Q_BLOCK, KEY_TILE)
            pv = _dot(p.astype(BF16), vaug_s[pl.ds(st, KEY_TILE), :])
            acc_s[sub, :, :ATTN_DIM] = alpha * acc_s[sub, :, :ATTN_DIM] + pv[:, :ATTN_DIM]
            acc_s[sub, :, ATTN_DIM:] = alpha * acc_s[sub, :, ATTN_DIM:] + pv[:, ATTN_DIM:]
            m_s[sub] = m_new

    def far_body(kt, carry):
        attn_tile(kt, (None,) * DSA_SUBS)
        return carry

    lax.fori_loop(0, qq - 1, far_body, 0)

    @pl.when(qq >= 1)
    def _():
        attn_tile(qq - 1, (2, None))

    attn_tile(qq, (0, 1))

    for sub in subs:
        out = acc_s[sub, :, :ATTN_DIM] / acc_s[sub, :, ATTN_DIM:]
        for hh in range(N_ATTN_HEADS):
            o_ref[sub * Q_BLOCK:(sub + 1) * Q_BLOCK, hh * ATTN_DIM:(hh + 1) * ATTN_DIM] = (
                out[hh * Q_BLOCK:(hh + 1) * Q_BLOCK].astype(BF16))


def _dsa_attention(p3, iw3, q_norm_g, k_norm_g, rel_bias):
    bsz, s, _ = p3.shape
    nq = s // Q_BLOCK
    qw = N_ATTN_HEADS * ATTN_DIM
    iqw = N_IDX_HEADS * IDX_DIM
    small0 = (qw + iqw) // LANES
    step_rows = DSA_SUBS * Q_BLOCK

    def small(off):
        return pl.BlockSpec((None, s, LANES), lambda b, i: (b, 0, small0 + off))

    return pl.pallas_call(
        functools.partial(_dsa_kernel, topk=min(MAX_TOPK, s // 4)),
        grid=(bsz, nq // DSA_SUBS),
        in_specs=[pl.BlockSpec(memory_space=pltpu.SMEM),
                  pl.BlockSpec((None, step_rows, qw), lambda b, i: (b, i, 0)),
                  pl.BlockSpec((None, step_rows, iqw), lambda b, i: (b, i, qw // iqw)),
                  pl.BlockSpec((None, step_rows, LANES), lambda b, i: (b, i, 0)),
                  small(0), small(1), small(2), small(3),
                  pl.BlockSpec((1, ATTN_DIM), lambda b, i: (0, 0)),
                  pl.BlockSpec((1, ATTN_DIM), lambda b, i: (0, 0))],
        out_specs=pl.BlockSpec((None, step_rows, qw), lambda b, i: (b, i, 0)),
        out_shape=jax.ShapeDtypeStruct((bsz, s, qw), BF16),
        scratch_shapes=[pltpu.VMEM((DSA_SUBS, s, 2 * ATTN_DIM), BF16),
                        pltpu.VMEM((nq, 2 * Q_BLOCK, LANES), BF16),
                        pltpu.VMEM((s, 2 * ATTN_DIM), BF16),
                        pltpu.VMEM((DSA_SUBS, s // KEY_TILE, KEY_TILE, Q_BLOCK), I32),
                        pltpu.VMEM((DSA_SUBS, s // KEY_TILE, KEY_TILE, Q_BLOCK), I16),
                        pltpu.VMEM((3, N_ATTN_HEADS, Q_BLOCK, KEY_TILE), F32),
                        pltpu.VMEM((DSA_SUBS, N_IDX_HEADS, Q_BLOCK, Q_BLOCK), F32),
                        pltpu.VMEM((DSA_SUBS, N_ATTN_HEADS * Q_BLOCK, 2 * ATTN_DIM), BF16),
                        pltpu.VMEM((DSA_SUBS * N_IDX_HEADS // 2 * Q_BLOCK, LANES), BF16),
                        pltpu.VMEM((DSA_SUBS, N_ATTN_HEADS * Q_BLOCK, 2 * ATTN_DIM), F32),
                        pltpu.VMEM((DSA_SUBS, N_ATTN_HEADS, Q_BLOCK, ATTN_DIM), F32)],
        compiler_params=_params("arbitrary", "arbitrary"),
        name="dsa_attention",
    )(rel_bias, p3, p3, iw3, p3, p3, p3, p3, q_norm_g.reshape(1, ATTN_DIM), k_norm_g.reshape(1, ATTN_DIM))


def _outproj_kernel(ya_ref, yb_ref, x_ref, w_ref, g1_ref, n2_ref, sc2_ref, sh2_ref, x1_ref, h2_ref):
    half = ya_ref.shape[1]
    chunk = x_ref.shape[0] // OUTPROJ_ROW_CHUNKS
    for r in range(OUTPROJ_ROW_CHUNKS):
        rows = slice(r * chunk, (r + 1) * chunk)
        mix = _dot(ya_ref[rows, :], w_ref[:half, :]) + _dot(yb_ref[rows, :], w_ref[half:, :])
        x1 = x_ref[rows, :] + g1_ref[...] * mix
        x1_ref[rows, :] = x1
        h2_ref[rows, :] = _norm_modulate(x1, n2_ref[...], sc2_ref[...], sh2_ref[...]).astype(BF16)


def _out_projection(ya, yb, yb_col, x2d, w_out, norm2_g, mod3, layer, bsz, tm):
    m, d = x2d.shape
    tpb = (m // bsz) // tm
    row = pl.BlockSpec((tm, d), lambda i: (i, 0))
    return pl.pallas_call(
        _outproj_kernel,
        grid=(m // tm,),
        in_specs=[pl.BlockSpec((tm, d // 2), lambda i: (i, 0)),
                  pl.BlockSpec((tm, d // 2), lambda i: (i, yb_col)),
                  row,
                  pl.BlockSpec((d, d), lambda i: (0, 0)),
                  _mod_spec(layer, 2, bsz, tpb),
                  pl.BlockSpec((1, d), lambda i: (0, 0)),
                  _mod_spec(layer, 4, bsz, tpb),
                  _mod_spec(layer, 3, bsz, tpb)],
        out_specs=[row, row],
        out_shape=[jax.ShapeDtypeStruct((m, d), F32), jax.ShapeDtypeStruct((m, d), BF16)],
        compiler_params=_params("arbitrary"),
        name=f"out_projection_{layer}",
    )(ya, yb, x2d, w_out, mod3, norm2_g.reshape(1, d), mod3, mod3)


def _mlp_kernel(h_ref, x1_ref, w1_ref, w2_ref, g2_ref, o_ref):
    j = pl.program_id(1)

    @pl.when(j == 0)
    def _():
        o_ref[...] = jnp.zeros_like(o_ref)

    a = jnp.square(jnp.maximum(_dot(h_ref[...], w1_ref[...]), 0.0)).astype(BF16)
    o_ref[...] += _dot(a, w2_ref[...])

    @pl.when(j == pl.num_programs(1) - 1)
    def _():
        o_ref[...] = x1_ref[...] + g2_ref[...] * o_ref[...]


def _mlp(h2, x1, w1, w2, mod3, layer, bsz, tm, tf):
    m, d = x1.shape
    ff = w1.shape[-1]
    tpb = (m // bsz) // tm
    row = pl.BlockSpec((tm, d), lambda i, j: (i, 0))
    return pl.pallas_call(
        _mlp_kernel,
        grid=(m // tm, ff // tf),
        in_specs=[row, row,
                  pl.BlockSpec((None, d, tf), lambda i, j: (layer, 0, j)),
                  pl.BlockSpec((None, tf, d), lambda i, j: (layer, j, 0)),
                  _mod_spec(layer, 5, bsz, tpb)],
        out_specs=row,
        out_shape=jax.ShapeDtypeStruct((m, d), F32),
        compiler_params=_params("arbitrary", "arbitrary"),
        name=f"mlp_{layer}",
    )(h2, x1, w1, w2, mod3)


def _pad_cols(w, width):
    return jnp.pad(w, ((0, 0), (0, width - w.shape[1])))


def _even_weights(w_in):
    gw = GROUP_WIDTH
    g0 = 4 * gw
    g1 = g0 + 2 * N_GROUP_HEADS
    main = jnp.concatenate([w_in[:, :g0], w_in[:, g1:]], axis=1).astype(BF16)
    tail = _pad_cols(w_in[:, g0:g1], LANES).astype(BF16)
    return main, tail


def _odd_weights(w_in):
    qw = N_ATTN_HEADS * ATTN_DIM
    iqw = N_IDX_HEADS * IDX_DIM
    o_k, o_v, o_iq = qw, qw + ATTN_DIM, qw + 2 * ATTN_DIM
    o_ik = o_iq + iqw
    o_iw = o_ik + IDX_DIM
    w_ik = w_in[:, o_ik:o_iw]
    zeros = jnp.zeros_like(w_ik)
    main = jnp.concatenate([w_in[:, :qw], w_in[:, o_iq:o_ik], w_in[:, o_k:o_v], w_in[:, o_v:o_iq],
                            w_ik, zeros, zeros, w_ik], axis=1).astype(BF16)
    tail = _pad_cols(w_in[:, o_iw:], LANES).astype(BF16)
    return main, tail


def kernel(x, c, ada_w, ada_b, norm1_g, norm2_g, mlp_w1, mlp_w2, even_w_in, even_conv_w, even_gate_b,
           even_head_norm_g, even_w_out, odd_w_in, odd_q_norm_g, odd_k_norm_g, odd_w_out, rel_bias):
    bsz, s, d = x.shape
    depth = ada_w.shape[0]
    m = bsz * s
    tm = ROW_TILE
    mod3 = _ada_modulation(c, ada_w, ada_b).reshape(depth * bsz * 6, 1, d)
    cos, sin = _rope_tables(s)
    w1_all = mlp_w1.astype(BF16)
    w2_all = mlp_w2.astype(BF16)
    xc = x.reshape(m, d)
    for l in range(depth):
        e = l // 2
        if l % 2 == 0:
            w_main, w_tail = _even_weights(even_w_in[e])
            p, gates = _in_projection(xc, norm1_g[l], mod3, l, bsz, w_main, w_tail, PROJ_ROW_TILE, EVEN_COL_TILE)
            p3 = p.reshape(bsz, s, -1)
            gates_col = gates.reshape(bsz, s, LANES)
            gates_row = jnp.swapaxes(
                gates_col[:, :, :2 * N_GROUP_HEADS].reshape(bsz, s // CHUNK, CHUNK, 2 * N_GROUP_HEADS), 2, 3)
            ya = _mlstm_heads(p3, gates_col, gates_row, even_conv_w[e], even_gate_b[e], even_head_norm_g[e])
            yb = _retention_heads(p3, cos, sin, even_head_norm_g[e])
            ya, yb, yb_col = ya.reshape(m, d // 2), yb.reshape(m, d // 2), 0
            w_out = even_w_out[e]
        else:
            w_main, w_tail = _odd_weights(odd_w_in[e])
            p, iw = _in_projection(xc, norm1_g[l], mod3, l, bsz, w_main, w_tail, PROJ_ROW_TILE, ODD_COL_TILE)
            y = _dsa_attention(p.reshape(bsz, s, -1), iw.reshape(bsz, s, LANES),
                               odd_q_norm_g[e], odd_k_norm_g[e], rel_bias)
            ya = yb = y.reshape(m, d)
            yb_col = 1
            w_out = odd_w_out[e]
        x1, h2 = _out_projection(ya, yb, yb_col, xc, w_out.astype(BF16), norm2_g[l], mod3, l, bsz, tm)
        xc = _mlp(h2, x1, w1_all, w2_all, mod3, l, bsz, tm, FF_TILE)
    return xc.reshape(bsz, s, d)
```

```python
import functools
import math

import numpy as np
import jax
import jax.numpy as jnp
from jax import lax
from jax.experimental import pallas as pl
from jax.experimental.pallas import tpu as pltpu

F32 = jnp.float32
BF16 = jnp.bfloat16
I32 = jnp.int32

D_MODEL = 2048
D_FF = 4 * D_MODEL
EPS = 1e-6
CHUNK = 128
HEAD_DIM = 256
N_GROUP_HEADS = 4
MLSTM_HEADS_PER_STEP = 2
RET_HEADS_PER_STEP = 4
GROUP_WIDTH = N_GROUP_HEADS * HEAD_DIM
CONV_WIDTH = 4
GATE_SOFTCAP = 15.0
ROPE_BASE = 10000.0
N_ATTN_HEADS = 16
ATTN_DIM = 128
N_IDX_HEADS = 16
IDX_DIM = 64
MAX_TOPK = 256
Q_BLOCK = 128
KEY_TILE = 2 * Q_BLOCK
DSA_SUBS = 2
assert DSA_SUBS * Q_BLOCK == KEY_TILE
REL_BUCKETS = 32
REL_MAX_DISTANCE = 128
LANES = 128
COUNT_ROWS = 64
INT_MIN = -(2 ** 31)
NEG_BIG = -1e30
LOG2E = math.log2(math.e)
VMEM_LIMIT = 56 * 1024 * 1024
ROW_TILE = 512
PROJ_ROW_TILE = 1024
FF_TILE = 1024
EVEN_COL_TILE = 2048
ODD_COL_TILE = 1792


def _t5_large_bucket_thresholds():
    max_exact = REL_BUCKETS // 2
    d = np.arange(0, 4 * REL_MAX_DISTANCE)
    large = max_exact + (np.log(np.maximum(d, 1) / max_exact) / math.log(REL_MAX_DISTANCE / max_exact)
                         * (REL_BUCKETS - max_exact)).astype(np.int64)
    bucket = np.where(d < max_exact, d, np.minimum(large, REL_BUCKETS - 1))
    return [int(d[bucket >= b].min()) for b in range(max_exact + 1, REL_BUCKETS)]


T5_THRESHOLDS = _t5_large_bucket_thresholds()
assert T5_THRESHOLDS[-1] <= Q_BLOCK + 1


def _params(*semantics):
    return pltpu.CompilerParams(dimension_semantics=semantics, vmem_limit_bytes=VMEM_LIMIT)


def _dot(a, b):
    return jnp.dot(a, b, preferred_element_type=F32)


def _dot_nt(a, b):
    return lax.dot_general(a, b, (((1,), (1,)), ((), ())), preferred_element_type=F32)


def _dot_tn(a, b):
    return lax.dot_general(a, b, (((0,), (0,)), ((), ())), preferred_element_type=F32)


def _sigmoid(x):
    return 1.0 / (1.0 + jnp.exp(-x))


def _log_sigmoid(x):
    return jnp.minimum(x, 0.0) - jnp.log1p(jnp.exp(-jnp.abs(x)))


def _softcap(x):
    return GATE_SOFTCAP * jnp.tanh(x / GATE_SOFTCAP)


def _norm_modulate(x, g, sc, sh):
    r = lax.rsqrt(jnp.mean(x * x, axis=-1, keepdims=True) + EPS)
    return (x * r * g) * (1.0 + sc) + sh


def _ada_kernel(c_ref, w_ref, b_ref, o_ref):
    c = c_ref[...]
    cond = c * _sigmoid(c)
    o_ref[...] = jnp.dot(cond, w_ref[...], preferred_element_type=F32,
                         precision=lax.Precision.HIGHEST) + b_ref[...]


def _ada_modulation(c, ada_w, ada_b):
    depth, d, n = ada_w.shape
    bsz = c.shape[0]
    tn = 1024
    return pl.pallas_call(
        _ada_kernel,
        grid=(depth, n // tn),
        in_specs=[pl.BlockSpec((bsz, d), lambda l, j: (0, 0)),
                  pl.BlockSpec((None, d, tn), lambda l, j: (l, 0, j)),
                  pl.BlockSpec((None, 1, tn), lambda l, j: (l, 0, j))],
        out_specs=pl.BlockSpec((None, bsz, tn), lambda l, j: (l, 0, j)),
        out_shape=jax.ShapeDtypeStruct((depth, bsz, n), F32),
        compiler_params=_params("arbitrary", "arbitrary"),
        name="ada_modulation",
    )(c, ada_w, ada_b.reshape(depth, 1, n))


def _mod_spec(layer, which, bsz, tiles_per_batch):
    return pl.BlockSpec((None, 1, D_MODEL),
                        lambda i, *_: ((layer * bsz + i // tiles_per_batch) * 6 + which, 0, 0))


def _proj_kernel(x_ref, g_ref, sc_ref, sh_ref, w_ref, wt_ref, p_ref, t_ref, h_s):
    @pl.when(pl.program_id(1) == 0)
    def _():
        h = _norm_modulate(x_ref[...], g_ref[...], sc_ref[...], sh_ref[...]).astype(BF16)
        h_s[...] = h
        t_ref[...] = _dot(h, wt_ref[...])

    p_ref[...] = _dot(h_s[...], w_ref[...]).astype(p_ref.dtype)


def _in_projection(x2d, norm_g, mod3, layer, bsz, w_main, w_tail, tm, tn):
    m, d = x2d.shape
    n = w_main.shape[1]
    tpb = (m // bsz) // tm
    return pl.pallas_call(
        _proj_kernel,
        grid=(m // tm, n // tn),
        in_specs=[pl.BlockSpec((tm, d), lambda i, j: (i, 0)),
                  pl.BlockSpec((1, d), lambda i, j: (0, 0)),
                  _mod_spec(layer, 1, bsz, tpb),
                  _mod_spec(layer, 0, bsz, tpb),
                  pl.BlockSpec((d, tn), lambda i, j: (0, j)),
                  pl.BlockSpec((d, LANES), lambda i, j: (0, 0))],
        out_specs=[pl.BlockSpec((tm, tn), lambda i, j: (i, j)),
                   pl.BlockSpec((tm, LANES), lambda i, j: (i, 0))],
        out_shape=[jax.ShapeDtypeStruct((m, n), BF16),
                   jax.ShapeDtypeStruct((m, LANES), F32)],
        scratch_shapes=[pltpu.VMEM((tm, d), BF16)],
        compiler_params=_params("arbitrary", "arbitrary"),
        name=f"in_projection_{layer}",
    )(x2d, norm_g.reshape(1, d), mod3, mod3, w_main, w_tail)


def _rope_kernel(cos_ref, sin_ref):
    s, half = cos_ref.shape
    pos = lax.broadcasted_iota(I32, (s, half), 0).astype(F32)
    idx = lax.broadcasted_iota(I32, (s, half), 1).astype(F32)
    inv = jnp.exp(idx * (-math.log(ROPE_BASE) / half))
    ang = pos * inv
    cos_ref[...] = jnp.cos(ang)
    sin_ref[...] = jnp.sin(ang)


def _rope_tables(s):
    half = HEAD_DIM // 2
    return pl.pallas_call(
        _rope_kernel,
        out_shape=[jax.ShapeDtypeStruct((s, half), F32)] * 2,
        name="rope_tables",
    )()


def _group_norm_gate(h, ng, gate):
    r = lax.rsqrt(jnp.mean(h * h, axis=-1, keepdims=True) + EPS)
    return (h * r * ng * gate).astype(BF16)


def _mlstm_kernel(gb_ref, q_ref, k_ref, v_ref, o_ref, cwq_ref, cwk_ref, gc_ref, gbrow_ref, gr_ref,
                  ng_ref, y_ref, q_s, k_s, c_s, n_s, m_s):
    hps = ng_ref.shape[0]
    h0 = pl.program_id(1) * hps
    s = q_ref.shape[0]
    rows = lax.broadcasted_iota(I32, (s, 1), 0)

    def conv_silu(x_ref, cw_ref):
        x = x_ref[...].astype(F32)
        acc = x * cw_ref[CONV_WIDTH - 1:CONV_WIDTH, :]
        for sft in range(1, CONV_WIDTH):
            xs = jnp.where(rows >= sft, pltpu.roll(x, sft, 0), 0.0)
            acc = acc + xs * cw_ref[CONV_WIDTH - 1 - sft:CONV_WIDTH - sft, :]
        return acc * _sigmoid(acc)

    q_s[...] = conv_silu(q_ref, cwq_ref).astype(BF16)
    k_s[...] = (conv_silu(k_ref, cwk_ref) * (HEAD_DIM ** -0.5)).astype(BF16)
    c_s[...] = jnp.zeros_like(c_s)
    n_s[...] = jnp.zeros_like(n_s)
    m_s[...] = jnp.zeros_like(m_s)

    lane = lax.broadcasted_iota(I32, (CHUNK, LANES), 1)
    ri = lax.broadcasted_iota(I32, (CHUNK, CHUNK), 0)
    ci = lax.broadcasted_iota(I32, (CHUNK, CHUNK), 1)
    tril = ri >= ci

    def head_chunk(c, st, gcap, j):
        h = h0 + j
        cols = slice(j * HEAD_DIM, (j + 1) * HEAD_DIM)
        li_col = jnp.sum(jnp.where(lane == h, gcap, 0.0), axis=1, keepdims=True)
        lf_col = jnp.sum(jnp.where(lane == h + N_GROUP_HEADS, _log_sigmoid(gcap), 0.0), axis=1, keepdims=True)
        li_row = _softcap(gr_ref[c, pl.ds(h, 1), :] + gb_ref[h])
        lf_row = _log_sigmoid(_softcap(gr_ref[c, pl.ds(h + N_GROUP_HEADS, 1), :] + gb_ref[h + N_GROUP_HEADS]))
        bcum_col = jnp.sum(jnp.where(tril, lf_row, 0.0), axis=1, keepdims=True)
        bcum_row = jnp.sum(jnp.where(ri <= ci, lf_col, 0.0), axis=0, keepdims=True)
        b_last = jnp.sum(lf_row, axis=1, keepdims=True)
        m_prev = m_s[j]

        dmat = jnp.where(tril, bcum_col - bcum_row + li_row, -jnp.inf)
        m_inter = bcum_col + m_prev
        m_row = jnp.maximum(jnp.max(dmat, axis=1, keepdims=True), m_inter)
        qc = q_s[pl.ds(st, CHUNK), cols]
        kc = k_s[pl.ds(st, CHUNK), cols]
        vc = v_ref[pl.ds(st, CHUNK), cols]
        sc = _dot_nt(qc, kc) * jnp.exp(dmat - m_row)
        inter = jnp.exp(m_inter - m_row)
        num = _dot(sc.astype(BF16), vc) + inter * _dot(qc, c_s[j].astype(BF16))
        den = (jnp.sum(sc, axis=1, keepdims=True)
               + inter * jnp.sum(qc.astype(F32) * n_s[j], axis=1, keepdims=True))
        h_out = num / jnp.maximum(jnp.abs(den), jnp.exp(-m_row))

        g_col = b_last - bcum_col + li_col
        m_new = jnp.maximum(b_last + m_prev, jnp.max(g_col, axis=0, keepdims=True))
        decay = jnp.exp(b_last + m_prev - m_new)
        kw = kc.astype(F32) * jnp.exp(g_col - m_new)
        c_s[j] = decay * c_s[j] + _dot_tn(kw.astype(BF16), vc)
        n_s[j] = decay * n_s[j] + jnp.sum(kw, axis=0, keepdims=True)
        m_s[j] = m_new

        gate = _sigmoid(o_ref[pl.ds(st, CHUNK), cols].astype(F32))
        y_ref[pl.ds(st, CHUNK), cols] = _group_norm_gate(h_out, ng_ref[j], gate)

    def body(c, carry):
        st = pl.multiple_of(c * CHUNK, CHUNK)
        gcap = _softcap(gc_ref[pl.ds(st, CHUNK), :] + gbrow_ref[...])
        for j in range(hps):
            head_chunk(c, st, gcap, j)
        return carry

    lax.fori_loop(0, s // CHUNK, body, 0)


def _mlstm_heads(p3, gates_col, gates_row, conv_w, gate_b, head_norm_g):
    bsz, s, _ = p3.shape
    nh = N_GROUP_HEADS
    hps = MLSTM_HEADS_PER_STEP
    nsteps = nh // hps
    w = hps * HEAD_DIM

    def col(group):
        return pl.BlockSpec((None, s, w), lambda b, h: (b, 0, group * nsteps + h))

    gb_row = jnp.zeros((1, LANES), F32).at[0, :2 * nh].set(gate_b)
    return pl.pallas_call(
        _mlstm_kernel,
        grid=(bsz, nsteps),
        in_specs=[pl.BlockSpec(memory_space=pltpu.SMEM),
                  col(0), col(1), col(2), col(3),
                  pl.BlockSpec((CONV_WIDTH, w), lambda b, h: (0, h)),
                  pl.BlockSpec((CONV_WIDTH, w), lambda b, h: (0, nsteps + h)),
                  pl.BlockSpec((None, s, LANES), lambda b, h: (b, 0, 0)),
                  pl.BlockSpec((1, LANES), lambda b, h: (0, 0)),
                  pl.BlockSpec((None, s // CHUNK, 2 * nh, CHUNK), lambda b, h: (b, 0, 0, 0)),
                  pl.BlockSpec((hps, 1, HEAD_DIM), lambda b, h: (h, 0, 0))],
        out_specs=pl.BlockSpec((None, s, w), lambda b, h: (b, 0, h)),
        out_shape=jax.ShapeDtypeStruct((bsz, s, GROUP_WIDTH), BF16),
        scratch_shapes=[pltpu.VMEM((s, w), BF16), pltpu.VMEM((s, w), BF16),
                        pltpu.VMEM((hps, HEAD_DIM, HEAD_DIM), F32), pltpu.VMEM((hps, 1, HEAD_DIM), F32),
                        pltpu.VMEM((hps, 1, 1), F32)],
        compiler_params=_params("arbitrary", "arbitrary"),
        name="mlstm_heads",
    )(gate_b, p3, p3, p3, p3, conv_w, conv_w, gates_col, gb_row, gates_row,
      head_norm_g.reshape(2 * nh, 1, HEAD_DIM))


def _ret_kernel(q_ref, k_ref, v_ref, g_ref, cos_ref, sin_ref, ng_ref, y_ref, q_s, k_s, r_s):
    hps = ng_ref.shape[0]
    h0 = pl.program_id(1) * hps
    s = q_ref.shape[0]
    half = HEAD_DIM // 2
    cos = cos_ref[...]
    sin = sin_ref[...]

    def rotate(x_ref, dst, scale):
        for j in range(hps):
            lo = slice(j * HEAD_DIM, j * HEAD_DIM + half)
            hi = slice(j * HEAD_DIM + half, (j + 1) * HEAD_DIM)
            x1 = x_ref[:, lo].astype(F32)
            x2 = x_ref[:, hi].astype(F32)
            dst[:, lo] = ((x1 * cos - x2 * sin) * scale).astype(BF16)
            dst[:, hi] = ((x2 * cos + x1 * sin) * scale).astype(BF16)

    rotate(q_ref, q_s, 1.0)
    rotate(k_ref, k_s, HEAD_DIM ** -0.5)
    r_s[...] = jnp.zeros_like(r_s)

    ri = lax.broadcasted_iota(I32, (CHUNK, CHUNK), 0)
    ci = lax.broadcasted_iota(I32, (CHUNK, CHUNK), 1)
    rel = (ri - ci).astype(F32)
    jcol = lax.broadcasted_iota(I32, (CHUNK, 1), 0).astype(F32)
    decays = []
    for j in range(hps):
        log_gamma = jnp.full((1, 1), math.log(1.0 - 2.0 ** -5.0), F32)
        for i in range(1, N_GROUP_HEADS):
            log_gamma = jnp.where(h0 + j == i, math.log(1.0 - 2.0 ** (-5.0 - i)), log_gamma)
        decays.append((jnp.where(rel >= 0, jnp.exp(jnp.maximum(rel, 0.0) * log_gamma), 0.0),
                       jnp.exp((jcol + 1.0) * log_gamma),
                       jnp.exp((CHUNK - 1.0 - jcol) * log_gamma),
                       jnp.exp(CHUNK * log_gamma)))

    def body(c, carry):
        st = pl.multiple_of(c * CHUNK, CHUNK)
        for j in range(hps):
            dmask, cross_decay, state_decay, chunk_decay = decays[j]
            cols = slice(j * HEAD_DIM, (j + 1) * HEAD_DIM)
            qc = q_s[pl.ds(st, CHUNK), cols]
            kc = k_s[pl.ds(st, CHUNK), cols]
            vc = v_ref[pl.ds(st, CHUNK), cols]
            inner = _dot((_dot_nt(qc, kc) * dmask).astype(BF16), vc)
            cross = _dot(qc, r_s[j].astype(BF16)) * cross_decay
            r_s[j] = chunk_decay * r_s[j] + _dot_tn((kc.astype(F32) * state_decay).astype(BF16), vc)
            g = g_ref[pl.ds(st, CHUNK), cols].astype(F32)
            y_ref[pl.ds(st, CHUNK), cols] = _group_norm_gate(inner + cross, ng_ref[j], g * _sigmoid(g))
        return carry

    lax.fori_loop(0, s // CHUNK, body, 0)


def _retention_heads(p3, cos, sin, head_norm_g):
    bsz, s, _ = p3.shape
    nh = N_GROUP_HEADS
    hps = RET_HEADS_PER_STEP
    nsteps = nh // hps
    w = hps * HEAD_DIM

    def col(group):
        return pl.BlockSpec((None, s, w), lambda b, h: (b, 0, group * nsteps + h))

    return pl.pallas_call(
        _ret_kernel,
        grid=(bsz, nsteps),
        in_specs=[col(4), col(5), col(6), col(7),
                  pl.BlockSpec((s, HEAD_DIM // 2), lambda b, h: (0, 0)),
                  pl.BlockSpec((s, HEAD_DIM // 2), lambda b, h: (0, 0)),
                  pl.BlockSpec((hps, 1, HEAD_DIM), lambda b, h: (nsteps + h, 0, 0))],
        out_specs=pl.BlockSpec((None, s, w), lambda b, h: (b, 0, h)),
        out_shape=jax.ShapeDtypeStruct((bsz, s, GROUP_WIDTH), BF16),
        scratch_shapes=[pltpu.VMEM((s, w), BF16), pltpu.VMEM((s, w), BF16),
                        pltpu.VMEM((hps, HEAD_DIM, HEAD_DIM), F32)],
        compiler_params=_params("arbitrary", "arbitrary"),
        name="retention_heads",
    )(p3, p3, p3, p3, cos, sin, head_norm_g.reshape(2 * nh, 1, HEAD_DIM))


def _order_key(x):
    bits = pltpu.bitcast(x, I32)
    return bits ^ ((bits >> 31) & 0x7FFFFFFF)


def _dsa_kernel(tab_ref, q_ref, iq_ref, iw_ref, k_ref, v_ref, ika_ref, ikb_ref, qg_ref, kg_ref, o_ref,
                kext_s, ikab_s, vaug_s, keyt_s, tb_s, wb_s, qall_s, iqall_s, acc_s, m_s, *, topk):
    b = pl.program_id(0)
    qq = pl.program_id(1)
    s = k_ref.shape[0]
    nkb = s // Q_BLOCK
    subs = range(DSA_SUBS)
    ri = lax.broadcasted_iota(I32, (Q_BLOCK, Q_BLOCK), 0)
    ci = lax.broadcasted_iota(I32, (Q_BLOCK, Q_BLOCK), 1)

    def qblock(sub):
        return DSA_SUBS * qq + sub

    @pl.when((b == 0) & (qq == 0))
    def _():
        buckets = []
        for back in range(2):
            dist = jnp.maximum(back * Q_BLOCK + ri - ci, 0)
            bucket = jnp.full(dist.shape, REL_BUCKETS // 2, I32)
            for thr in T5_THRESHOLDS:
                bucket = bucket + jnp.where(dist >= thr, 1, 0)
            buckets.append(jnp.where(dist < REL_BUCKETS // 2, dist, bucket))

        def head_body(hh, carry):
            far = tab_ref[REL_BUCKETS - 1, hh]
            tiles = []
            for bucket in buckets:
                tile = jnp.zeros(bucket.shape, F32)
                for bk in range(REL_BUCKETS):
                    tile = jnp.where(bucket == bk, (tab_ref[bk, hh] - far) * LOG2E, tile)
                tiles.append(tile)
            zeros = jnp.zeros((Q_BLOCK, Q_BLOCK), F32)
            for i, (first, second) in enumerate(((tiles[0], zeros), (tiles[1], tiles[0]), (zeros, tiles[1]))):
                tb_s[i, hh, :, :Q_BLOCK] = first
                tb_s[i, hh, :, Q_BLOCK:] = second
            return carry

        lax.fori_loop(0, N_ATTN_HEADS, head_body, 0)
        onehot = jnp.where(ri == ci, 1.0, 0.0).astype(BF16)
        for sub in subs:
            for hh in range(N_ATTN_HEADS):
                qall_s[sub, hh * Q_BLOCK:(hh + 1) * Q_BLOCK, ATTN_DIM:] = onehot

    @pl.when(qq == 0)
    def _():
        k = k_ref[...].astype(F32)
        r = lax.rsqrt(jnp.mean(k * k, axis=-1, keepdims=True) + EPS)
        kn = (k * r * kg_ref[...]).astype(BF16)
        for sub in subs:
            kext_s[sub, :, :ATTN_DIM] = kn
        for kt in range(nkb):
            ikab_s[kt, :Q_BLOCK, :] = ika_ref[kt * Q_BLOCK:(kt + 1) * Q_BLOCK, :]
            ikab_s[kt, Q_BLOCK:, :] = ikb_ref[kt * Q_BLOCK:(kt + 1) * Q_BLOCK, :]
        vaug_s[:, :ATTN_DIM] = v_ref[...]
        vaug_s[:, ATTN_DIM:] = jnp.ones((s, ATTN_DIM), BF16)

    ones = jnp.ones((ATTN_DIM, ATTN_DIM), BF16)
    pair_rows = N_IDX_HEADS // 2 * Q_BLOCK
    for sub in subs:
        rows = slice(sub * Q_BLOCK, (sub + 1) * Q_BLOCK)
        qf = jnp.concatenate([q_ref[rows, hh * ATTN_DIM:(hh + 1) * ATTN_DIM] for hh in range(N_ATTN_HEADS)],
                             axis=0).astype(F32)
        sq = qf * qf
        sq_hi = sq.astype(BF16)
        sq_lo = (sq - sq_hi.astype(F32)).astype(BF16)
        mean_sq = (_dot(sq_hi, ones) + _dot(sq_lo, ones)) * (1.0 / ATTN_DIM)
        qall_s[sub, :, :ATTN_DIM] = (qf * lax.rsqrt(mean_sq + EPS)
                                     * (qg_ref[...] * (ATTN_DIM ** -0.5 * LOG2E))).astype(BF16)
        for hp in range(N_IDX_HEADS // 2):
            iqall_s[sub * pair_rows + hp * Q_BLOCK:sub * pair_rows + (hp + 1) * Q_BLOCK, :] = (
                iq_ref[rows, hp * LANES:(hp + 1) * LANES])
        wv = iw_ref[rows, :] * ((N_IDX_HEADS ** -0.5) * (IDX_DIM ** -0.5))
        for hh in range(N_IDX_HEADS):
            wb_s[sub, hh] = jnp.broadcast_to(wv[:, hh:hh + 1], (Q_BLOCK, Q_BLOCK))

    n_tiles = qq + 1

    def score_body(kt, carry):
        for half in range(KEY_TILE // Q_BLOCK):
            kb = 2 * kt + half
            r = _dot_nt(iqall_s[...], ikab_s[kb])
            for sub in subs:
                acc = jnp.zeros((Q_BLOCK, Q_BLOCK), F32)
                for hp in range(N_IDX_HEADS // 2):
                    blk = r[sub * pair_rows + hp * Q_BLOCK:sub * pair_rows + (hp + 1) * Q_BLOCK]
                    acc = acc + jnp.maximum(blk[:, :Q_BLOCK], 0.0) * wb_s[sub, 2 * hp]
                    acc = acc + jnp.maximum(blk[:, Q_BLOCK:], 0.0) * wb_s[sub, 2 * hp + 1]
                causal_t = (kb * Q_BLOCK + ri) <= (qblock(sub) * Q_BLOCK + ci)
                keyt_s[sub, kt, half * Q_BLOCK:(half + 1) * Q_BLOCK, :] = jnp.where(
                    causal_t, _order_key(acc.T), INT_MIN)
        return carry

    lax.fori_loop(0, n_tiles, score_body, 0)

    def count(pred):
        def inner(kt, accs):
            out = []
            for sub in subs:
                hit = jnp.where(pred(keyt_s[sub, kt], sub), 1.0, 0.0)
                out.append(accs[sub] + jnp.sum(hit.reshape(KEY_TILE // COUNT_ROWS, COUNT_ROWS, Q_BLOCK), axis=0))
            return tuple(out)
        accs = lax.fori_loop(0, n_tiles, inner, tuple(jnp.zeros((COUNT_ROWS, Q_BLOCK), F32) for _ in subs))
        return tuple(jnp.sum(acc, axis=0, keepdims=True) for acc in accs)

    def count_ge(cands):
        return count(lambda keys, sub: keys >= cands[sub])

    zero = jnp.zeros((1, Q_BLOCK), I32)
    thr0 = tuple(jnp.where(n >= topk, zero, INT_MIN) for n in count_ge((zero,) * DSA_SUBS))

    def bit_body(i, thrs):
        cands = tuple(t + jnp.left_shift(jnp.int32(1), 30 - i) for t in thrs)
        counts = count_ge(cands)
        return tuple(jnp.where(counts[sub] >= topk, cands[sub], thrs[sub]) for sub in subs)

    thrs = lax.fori_loop(0, 31, bit_body, thr0)
    thrs = tuple(jnp.maximum(t, INT_MIN + 1) for t in thrs)

    tied_over = sum(jnp.where(n > topk, 1.0, 0.0) for n in count_ge(thrs))

    @pl.when(jnp.max(tied_over) > 0.0)
    def _():
        greater = count(lambda keys, sub: keys > thrs[sub])
        kr = lax.broadcasted_iota(I32, (KEY_TILE, KEY_TILE), 0)
        kc = lax.broadcasted_iota(I32, (KEY_TILE, KEY_TILE), 1)
        lower = jnp.where(kc <= kr, 1.0, 0.0).astype(BF16)

        def retire(kt, seen):
            out = []
            for sub in subs:
                keys = keyt_s[sub, kt]
                tied = keys == thrs[sub]
                prefix = _dot(lower, jnp.where(tied, 1.0, 0.0).astype(BF16)) + seen[sub]
                keyt_s[sub, kt] = jnp.where(tied & (prefix > topk - greater[sub]), INT_MIN, keys)
                out.append(prefix[KEY_TILE - 1:, :])
            return tuple(out)

        lax.fori_loop(0, n_tiles, retire, tuple(jnp.zeros((1, Q_BLOCK), F32) for _ in subs))

    acc_s[...] = jnp.zeros_like(acc_s)
    m_s[...] = jnp.full(m_s.shape, NEG_BIG, F32)

    def attn_tile(kt, biases):
        st = pl.multiple_of(kt * KEY_TILE, KEY_TILE)
        for sub in subs:
            kext_s[sub, pl.ds(st, KEY_TILE), ATTN_DIM:] = jnp.where(
                keyt_s[sub, kt] >= thrs[sub], 0.0, NEG_BIG).astype(BF16)
        for sub in subs:
            sc = _dot_nt(qall_s[sub], kext_s[sub, pl.ds(st, KEY_TILE), :])
            sc = sc.reshape(N_ATTN_HEADS, Q_BLOCK, KEY_TILE)
            if biases[sub] is not None:
                sc = sc + tb_s[biases[sub]]
            m_prev = m_s[sub]
            m_new = jnp.maximum(m_prev, jnp.max(sc, axis=-1, keepdims=True))
            alpha = jnp.exp2(m_prev - m_new).reshape(N_ATTN_HEADS * Q_BLOCK, ATTN_DIM)
            p = jnp.concatenate([jnp.exp2(sc[:, :, i * ATTN_DIM:(i + 1) * ATTN_DIM] - m_new)
                                 for i in range(KEY_TILE // ATTN_DIM)], axis=-1)
            p = p.reshape(N_ATTN_HEADS * Q_BLOCK, KEY_TILE)
            pv = _dot(p.astype(BF16), vaug_s[pl.ds(st, KEY_TILE), :])
            acc_s[sub, :, :ATTN_DIM] = alpha * acc_s[sub, :, :ATTN_DIM] + pv[:, :ATTN_DIM]
            acc_s[sub, :, ATTN_DIM:] = alpha * acc_s[sub, :, ATTN_DIM:] + pv[:, ATTN_DIM:]
            m_s[sub] = m_new

    def far_body(kt, carry):
        attn_tile(kt, (None,) * DSA_SUBS)
        return carry

    lax.fori_loop(0, qq - 1, far_body, 0)

    @pl.when(qq >= 1)
    def _():
        attn_tile(qq - 1, (2, None))

    attn_tile(qq, (0, 1))

    for sub in subs:
        out = acc_s[sub, :, :ATTN_DIM] / acc_s[sub, :, ATTN_DIM:]
        for hh in range(N_ATTN_HEADS):
            o_ref[sub * Q_BLOCK:(sub + 1) * Q_BLOCK, hh * ATTN_DIM:(hh + 1) * ATTN_DIM] = (
                out[hh * Q_BLOCK:(hh + 1) * Q_BLOCK].astype(BF16))


def _dsa_attention(p3, iw3, q_norm_g, k_norm_g, rel_bias):
    bsz, s, _ = p3.shape
    nq = s // Q_BLOCK
    qw = N_ATTN_HEADS * ATTN_DIM
    iqw = N_IDX_HEADS * IDX_DIM
    small0 = (qw + iqw) // LANES
    step_rows = DSA_SUBS * Q_BLOCK

    def small(off):
        return pl.BlockSpec((None, s, LANES), lambda b, i: (b, 0, small0 + off))

    return pl.pallas_call(
        functools.partial(_dsa_kernel, topk=min(MAX_TOPK, s // 4)),
        grid=(bsz, nq // DSA_SUBS),
        in_specs=[pl.BlockSpec(memory_space=pltpu.SMEM),
                  pl.BlockSpec((None, step_rows, qw), lambda b, i: (b, i, 0)),
                  pl.BlockSpec((None, step_rows, iqw), lambda b, i: (b, i, qw // iqw)),
                  pl.BlockSpec((None, step_rows, LANES), lambda b, i: (b, i, 0)),
                  small(0), small(1), small(2), small(3),
                  pl.BlockSpec((1, ATTN_DIM), lambda b, i: (0, 0)),
                  pl.BlockSpec((1, ATTN_DIM), lambda b, i: (0, 0))],
        out_specs=pl.BlockSpec((None, step_rows, qw), lambda b, i: (b, i, 0)),
        out_shape=jax.ShapeDtypeStruct((bsz, s, qw), BF16),
        scratch_shapes=[pltpu.VMEM((DSA_SUBS, s, 2 * ATTN_DIM), BF16),
                        pltpu.VMEM((nq, 2 * Q_BLOCK, LANES), BF16),
                        pltpu.VMEM((s, 2 * ATTN_DIM), BF16),
                        pltpu.VMEM((DSA_SUBS, s // KEY_TILE, KEY_TILE, Q_BLOCK), I32),
                        pltpu.VMEM((3, N_ATTN_HEADS, Q_BLOCK, KEY_TILE), F32),
                        pltpu.VMEM((DSA_SUBS, N_IDX_HEADS, Q_BLOCK, Q_BLOCK), F32),
                        pltpu.VMEM((DSA_SUBS, N_ATTN_HEADS * Q_BLOCK, 2 * ATTN_DIM), BF16),
                        pltpu.VMEM((DSA_SUBS * N_IDX_HEADS // 2 * Q_BLOCK, LANES), BF16),
                        pltpu.VMEM((DSA_SUBS, N_ATTN_HEADS * Q_BLOCK, 2 * ATTN_DIM), F32),
                        pltpu.VMEM((DSA_SUBS, N_ATTN_HEADS, Q_BLOCK, ATTN_DIM), F32)],
        compiler_params=_params("arbitrary", "arbitrary"),
        name="dsa_attention",
    )(rel_bias, p3, p3, iw3, p3, p3, p3, p3, q_norm_g.reshape(1, ATTN_DIM), k_norm_g.reshape(1, ATTN_DIM))


def _outproj_kernel(ya_ref, yb_ref, x_ref, w_ref, g1_ref, n2_ref, sc2_ref, sh2_ref, x1_ref, h2_ref):
    half = ya_ref.shape[1]
    mix = _dot(ya_ref[...], w_ref[:half, :]) + _dot(yb_ref[...], w_ref[half:, :])
    x1 = x_ref[...] + g1_ref[...] * mix
    x1_ref[...] = x1
    h2_ref[...] = _norm_modulate(x1, n2_ref[...], sc2_ref[...], sh2_ref[...]).astype(BF16)


def _out_projection(ya, yb, yb_col, x2d, w_out, norm2_g, mod3, layer, bsz, tm):
    m, d = x2d.shape
    tpb = (m // bsz) // tm
    row = pl.BlockSpec((tm, d), lambda i: (i, 0))
    return pl.pallas_call(
        _outproj_kernel,
        grid=(m // tm,),
        in_specs=[pl.BlockSpec((tm, d // 2), lambda i: (i, 0)),
                  pl.BlockSpec((tm, d // 2), lambda i: (i, yb_col)),
                  row,
                  pl.BlockSpec((d, d), lambda i: (0, 0)),
                  _mod_spec(layer, 2, bsz, tpb),
                  pl.BlockSpec((1, d), lambda i: (0, 0)),
                  _mod_spec(layer, 4, bsz, tpb),
                  _mod_spec(layer, 3, bsz, tpb)],
        out_specs=[row, row],
        out_shape=[jax.ShapeDtypeStruct((m, d), F32), jax.ShapeDtypeStruct((m, d), BF16)],
        compiler_params=_params("arbitrary"),
        name=f"out_projection_{layer}",
    )(ya, yb, x2d, w_out, mod3, norm2_g.reshape(1, d), mod3, mod3)


def _mlp_kernel(h_ref, x1_ref, w1_ref, w2_ref, g2_ref, o_ref):
    j = pl.program_id(1)

    @pl.when(j == 0)
    def _():
        o_ref[...] = jnp.zeros_like(o_ref)

    a = jnp.square(jnp.maximum(_dot(h_ref[...], w1_ref[...]), 0.0)).astype(BF16)
    o_ref[...] += _dot(a, w2_ref[...])

    @pl.when(j == pl.num_programs(1) - 1)
    def _():
        o_ref[...] = x1_ref[...] + g2_ref[...] * o_ref[...]


def _mlp(h2, x1, w1, w2, mod3, layer, bsz, tm, tf):
    m, d = x1.shape
    ff = w1.shape[-1]
    tpb = (m // bsz) // tm
    row = pl.BlockSpec((tm, d), lambda i, j: (i, 0))
    return pl.pallas_call(
        _mlp_kernel,
        grid=(m // tm, ff // tf),
        in_specs=[row, row,
                  pl.BlockSpec((None, d, tf), lambda i, j: (layer, 0, j)),
                  pl.BlockSpec((None, tf, d), lambda i, j: (layer, j, 0)),
                  _mod_spec(layer, 5, bsz, tpb)],
        out_specs=row,
        out_shape=jax.ShapeDtypeStruct((m, d), F32),
        compiler_params=_params("arbitrary", "arbitrary"),
        name=f"mlp_{layer}",
    )(h2, x1, w1, w2, mod3)


def _pad_cols(w, width):
    return jnp.pad(w, ((0, 0), (0, width - w.shape[1])))


def _even_weights(w_in):
    gw = GROUP_WIDTH
    g0 = 4 * gw
    g1 = g0 + 2 * N_GROUP_HEADS
    main = jnp.concatenate([w_in[:, :g0], w_in[:, g1:]], axis=1).astype(BF16)
    tail = _pad_cols(w_in[:, g0:g1], LANES).astype(BF16)
    return main, tail


def _odd_weights(w_in):
    qw = N_ATTN_HEADS * ATTN_DIM
    iqw = N_IDX_HEADS * IDX_DIM
    o_k, o_v, o_iq = qw, qw + ATTN_DIM, qw + 2 * ATTN_DIM
    o_ik = o_iq + iqw
    o_iw = o_ik + IDX_DIM
    w_ik = w_in[:, o_ik:o_iw]
    zeros = jnp.zeros_like(w_ik)
    main = jnp.concatenate([w_in[:, :qw], w_in[:, o_iq:o_ik], w_in[:, o_k:o_v], w_in[:, o_v:o_iq],
                            w_ik, zeros, zeros, w_ik], axis=1).astype(BF16)
    tail = _pad_cols(w_in[:, o_iw:], LANES).astype(BF16)
    return main, tail


def kernel(x, c, ada_w, ada_b, norm1_g, norm2_g, mlp_w1, mlp_w2, even_w_in, even_conv_w, even_gate_b,
           even_head_norm_g, even_w_out, odd_w_in, odd_q_norm_g, odd_k_norm_g, odd_w_out, rel_bias):
    bsz, s, d = x.shape
    depth = ada_w.shape[0]
    m = bsz * s
    tm = ROW_TILE
    mod3 = _ada_modulation(c, ada_w, ada_b).reshape(depth * bsz * 6, 1, d)
    cos, sin = _rope_tables(s)
    w1_all = mlp_w1.astype(BF16)
    w2_all = mlp_w2.astype(BF16)
    xc = x.reshape(m, d)
    for l in range(depth):
        e = l // 2
        if l % 2 == 0:
            w_main, w_tail = _even_weights(even_w_in[e])
            p, gates = _in_projection(xc, norm1_g[l], mod3, l, bsz, w_main, w_tail, PROJ_ROW_TILE, EVEN_COL_TILE)
            p3 = p.reshape(bsz, s, -1)
            gates_col = gates.reshape(bsz, s, LANES)
            gates_row = jnp.swapaxes(
                gates_col[:, :, :2 * N_GROUP_HEADS].reshape(bsz, s // CHUNK, CHUNK, 2 * N_GROUP_HEADS), 2, 3)
            ya = _mlstm_heads(p3, gates_col, gates_row, even_conv_w[e], even_gate_b[e], even_head_norm_g[e])
            yb = _retention_heads(p3, cos, sin, even_head_norm_g[e])
            ya, yb, yb_col = ya.reshape(m, d // 2), yb.reshape(m, d // 2), 0
            w_out = even_w_out[e]
        else:
            w_main, w_tail = _odd_weights(odd_w_in[e])
            p, iw = _in_projection(xc, norm1_g[l], mod3, l, bsz, w_main, w_tail, PROJ_ROW_TILE, ODD_COL_TILE)
            y = _dsa_attention(p.reshape(bsz, s, -1), iw.reshape(bsz, s, LANES),
                               odd_q_norm_g[e], odd_k_norm_g[e], rel_bias)
            ya = yb = y.reshape(m, d)
            yb_col = 1
            w_out = odd_w_out[e]
        x1, h2 = _out_projection(ya, yb, yb_col, xc, w_out.astype(BF16), norm2_g[l], mod3, l, bsz, tm)
        xc = _mlp(h2, x1, w1_all, w2_all, mod3, l, bsz, tm, FF_TILE)
    return xc.reshape(bsz, s, d)
```

```python
import functools
import math

import numpy as np
import jax
import jax.numpy as jnp
from jax import lax
from jax.experimental import pallas as pl
from jax.experimental.pallas import tpu as pltpu

F32 = jnp.float32
BF16 = jnp.bfloat16
I32 = jnp.int32

D_MODEL = 2048
D_FF = 4 * D_MODEL
EPS = 1e-6
CHUNK = 128
HEAD_DIM = 256
N_GROUP_HEADS = 4
MLSTM_HEADS_PER_STEP = 2
RET_HEADS_PER_STEP = 4
RET_CHUNK_UNROLL = 4
GROUP_WIDTH = N_GROUP_HEADS * HEAD_DIM
CONV_WIDTH = 4
GATE_SOFTCAP = 15.0
ROPE_BASE = 10000.0
N_ATTN_HEADS = 16
ATTN_DIM = 128
N_IDX_HEADS = 16
IDX_DIM = 64
MAX_TOPK = 256
Q_BLOCK = 128
KEY_TILE = 2 * Q_BLOCK
DSA_SUBS = 2
assert DSA_SUBS * Q_BLOCK == KEY_TILE
REL_BUCKETS = 32
REL_MAX_DISTANCE = 128
LANES = 128
COUNT_ROWS = 64
INT_MIN = -(2 ** 31)
NEG_BIG = -1e30
LOG2E = math.log2(math.e)
VMEM_LIMIT = 56 * 1024 * 1024
ROW_TILE = 512
PROJ_ROW_TILE = 1024
FF_TILE = 1024
EVEN_COL_TILE = 2048
ODD_COL_TILE = 1792


def _t5_large_bucket_thresholds():
    max_exact = REL_BUCKETS // 2
    d = np.arange(0, 4 * REL_MAX_DISTANCE)
    large = max_exact + (np.log(np.maximum(d, 1) / max_exact) / math.log(REL_MAX_DISTANCE / max_exact)
                         * (REL_BUCKETS - max_exact)).astype(np.int64)
    bucket = np.where(d < max_exact, d, np.minimum(large, REL_BUCKETS - 1))
    return [int(d[bucket >= b].min()) for b in range(max_exact + 1, REL_BUCKETS)]


T5_THRESHOLDS = _t5_large_bucket_thresholds()
assert T5_THRESHOLDS[-1] <= Q_BLOCK + 1


def _params(*semantics):
    return pltpu.CompilerParams(dimension_semantics=semantics, vmem_limit_bytes=VMEM_LIMIT)


def _dot(a, b):
    return jnp.dot(a, b, preferred_element_type=F32)


def _dot_nt(a, b):
    return lax.dot_general(a, b, (((1,), (1,)), ((), ())), preferred_element_type=F32)


def _dot_tn(a, b):
    return lax.dot_general(a, b, (((0,), (0,)), ((), ())), preferred_element_type=F32)


def _sigmoid(x):
    return 1.0 / (1.0 + jnp.exp(-x))


def _log_sigmoid(x):
    return jnp.minimum(x, 0.0) - jnp.log1p(jnp.exp(-jnp.abs(x)))


def _softcap(x):
    return GATE_SOFTCAP * jnp.tanh(x / GATE_SOFTCAP)


def _norm_modulate(x, g, sc, sh):
    r = lax.rsqrt(jnp.mean(x * x, axis=-1, keepdims=True) + EPS)
    return (x * r * g) * (1.0 + sc) + sh


def _ada_kernel(c_ref, w_ref, b_ref, o_ref):
    c = c_ref[...]
    cond = c * _sigmoid(c)
    o_ref[...] = jnp.dot(cond, w_ref[...], preferred_element_type=F32,
                         precision=lax.Precision.HIGHEST) + b_ref[...]


def _ada_modulation(c, ada_w, ada_b):
    depth, d, n = ada_w.shape
    bsz = c.shape[0]
    tn = 1024
    return pl.pallas_call(
        _ada_kernel,
        grid=(depth, n // tn),
        in_specs=[pl.BlockSpec((bsz, d), lambda l, j: (0, 0)),
                  pl.BlockSpec((None, d, tn), lambda l, j: (l, 0, j)),
                  pl.BlockSpec((None, 1, tn), lambda l, j: (l, 0, j))],
        out_specs=pl.BlockSpec((None, bsz, tn), lambda l, j: (l, 0, j)),
        out_shape=jax.ShapeDtypeStruct((depth, bsz, n), F32),
        compiler_params=_params("arbitrary", "arbitrary"),
        name="ada_modulation",
    )(c, ada_w, ada_b.reshape(depth, 1, n))


def _mod_spec(layer, which, bsz, tiles_per_batch):
    return pl.BlockSpec((None, 1, D_MODEL),
                        lambda i, *_: ((layer * bsz + i // tiles_per_batch) * 6 + which, 0, 0))


def _proj_kernel(x_ref, g_ref, sc_ref, sh_ref, w_ref, wt_ref, p_ref, t_ref, h_s):
    @pl.when(pl.program_id(1) == 0)
    def _():
        h = _norm_modulate(x_ref[...], g_ref[...], sc_ref[...], sh_ref[...]).astype(BF16)
        h_s[...] = h
        t_ref[...] = _dot(h, wt_ref[...])

    p_ref[...] = _dot(h_s[...], w_ref[...]).astype(p_ref.dtype)


def _in_projection(x2d, norm_g, mod3, layer, bsz, w_main, w_tail, tm, tn):
    m, d = x2d.shape
    n = w_main.shape[1]
    tpb = (m // bsz) // tm
    return pl.pallas_call(
        _proj_kernel,
        grid=(m // tm, n // tn),
        in_specs=[pl.BlockSpec((tm, d), lambda i, j: (i, 0)),
                  pl.BlockSpec((1, d), lambda i, j: (0, 0)),
                  _mod_spec(layer, 1, bsz, tpb),
                  _mod_spec(layer, 0, bsz, tpb),
                  pl.BlockSpec((d, tn), lambda i, j: (0, j)),
                  pl.BlockSpec((d, LANES), lambda i, j: (0, 0))],
        out_specs=[pl.BlockSpec((tm, tn), lambda i, j: (i, j)),
                   pl.BlockSpec((tm, LANES), lambda i, j: (i, 0))],
        out_shape=[jax.ShapeDtypeStruct((m, n), BF16),
                   jax.ShapeDtypeStruct((m, LANES), F32)],
        scratch_shapes=[pltpu.VMEM((tm, d), BF16)],
        compiler_params=_params("arbitrary", "arbitrary"),
        name=f"in_projection_{layer}",
    )(x2d, norm_g.reshape(1, d), mod3, mod3, w_main, w_tail)


def _rope_kernel(cos_ref, sin_ref):
    s, half = cos_ref.shape
    pos = lax.broadcasted_iota(I32, (s, half), 0).astype(F32)
    idx = lax.broadcasted_iota(I32, (s, half), 1).astype(F32)
    inv = jnp.exp(idx * (-math.log(ROPE_BASE) / half))
    ang = pos * inv
    cos_ref[...] = jnp.cos(ang)
    sin_ref[...] = jnp.sin(ang)


def _rope_tables(s):
    half = HEAD_DIM // 2
    return pl.pallas_call(
        _rope_kernel,
        out_shape=[jax.ShapeDtypeStruct((s, half), F32)] * 2,
        name="rope_tables",
    )()


def _group_norm_gate(h, ng, gate):
    r = lax.rsqrt(jnp.mean(h * h, axis=-1, keepdims=True) + EPS)
    return (h * r * ng * gate).astype(BF16)


def _mlstm_kernel(gb_ref, q_ref, k_ref, v_ref, o_ref, cwq_ref, cwk_ref, gc_ref, gbrow_ref, gr_ref,
                  ng_ref, y_ref, q_s, k_s, c_s, n_s, m_s):
    hps = ng_ref.shape[0]
    h0 = pl.program_id(1) * hps
    s = q_ref.shape[0]
    rows = lax.broadcasted_iota(I32, (s, 1), 0)

    def conv_silu(x_ref, cw_ref):
        x = x_ref[...].astype(F32)
        acc = x * cw_ref[CONV_WIDTH - 1:CONV_WIDTH, :]
        for sft in range(1, CONV_WIDTH):
            xs = jnp.where(rows >= sft, pltpu.roll(x, sft, 0), 0.0)
            acc = acc + xs * cw_ref[CONV_WIDTH - 1 - sft:CONV_WIDTH - sft, :]
        return acc * _sigmoid(acc)

    q_s[...] = conv_silu(q_ref, cwq_ref).astype(BF16)
    k_s[...] = (conv_silu(k_ref, cwk_ref) * (HEAD_DIM ** -0.5)).astype(BF16)
    c_s[...] = jnp.zeros_like(c_s)
    n_s[...] = jnp.zeros_like(n_s)
    m_s[...] = jnp.zeros_like(m_s)

    lane = lax.broadcasted_iota(I32, (CHUNK, LANES), 1)
    ri = lax.broadcasted_iota(I32, (CHUNK, CHUNK), 0)
    ci = lax.broadcasted_iota(I32, (CHUNK, CHUNK), 1)
    tril = ri >= ci

    def head_chunk(c, st, gcap, j):
        h = h0 + j
        cols = slice(j * HEAD_DIM, (j + 1) * HEAD_DIM)
        li_col = jnp.sum(jnp.where(lane == h, gcap, 0.0), axis=1, keepdims=True)
        lf_col = jnp.sum(jnp.where(lane == h + N_GROUP_HEADS, _log_sigmoid(gcap), 0.0), axis=1, keepdims=True)
        li_row = _softcap(gr_ref[c, pl.ds(h, 1), :] + gb_ref[h])
        lf_row = _log_sigmoid(_softcap(gr_ref[c, pl.ds(h + N_GROUP_HEADS, 1), :] + gb_ref[h + N_GROUP_HEADS]))
        bcum_col = jnp.sum(jnp.where(tril, lf_row, 0.0), axis=1, keepdims=True)
        bcum_row = jnp.sum(jnp.where(ri <= ci, lf_col, 0.0), axis=0, keepdims=True)
        b_last = jnp.sum(lf_row, axis=1, keepdims=True)
        m_prev = m_s[j]

        dmat = jnp.where(tril, bcum_col - bcum_row + li_row, -jnp.inf)
        m_inter = bcum_col + m_prev
        m_row = jnp.maximum(jnp.max(dmat, axis=1, keepdims=True), m_inter)
        qc = q_s[pl.ds(st, CHUNK), cols]
        kc = k_s[pl.ds(st, CHUNK), cols]
        vc = v_ref[pl.ds(st, CHUNK), cols]
        sc = _dot_nt(qc, kc) * jnp.exp(dmat - m_row)
        inter = jnp.exp(m_inter - m_row)
        num = _dot(sc.astype(BF16), vc) + inter * _dot(qc, c_s[j].astype(BF16))
        den = (jnp.sum(sc, axis=1, keepdims=True)
               + inter * jnp.sum(qc.astype(F32) * n_s[j], axis=1, keepdims=True))
        h_out = num / jnp.maximum(jnp.abs(den), jnp.exp(-m_row))

        g_col = b_last - bcum_col + li_col
        m_new = jnp.maximum(b_last + m_prev, jnp.max(g_col, axis=0, keepdims=True))
        decay = jnp.exp(b_last + m_prev - m_new)
        kw = kc.astype(F32) * jnp.exp(g_col - m_new)
        c_s[j] = decay * c_s[j] + _dot_tn(kw.astype(BF16), vc)
        n_s[j] = decay * n_s[j] + jnp.sum(kw, axis=0, keepdims=True)
        m_s[j] = m_new

        gate = _sigmoid(o_ref[pl.ds(st, CHUNK), cols].astype(F32))
        y_ref[pl.ds(st, CHUNK), cols] = _group_norm_gate(h_out, ng_ref[j], gate)

    def body(c, carry):
        st = pl.multiple_of(c * CHUNK, CHUNK)
        gcap = _softcap(gc_ref[pl.ds(st, CHUNK), :] + gbrow_ref[...])
        for j in range(hps):
            head_chunk(c, st, gcap, j)
        return carry

    lax.fori_loop(0, s // CHUNK, body, 0)


def _mlstm_heads(p3, gates_col, gates_row, conv_w, gate_b, head_norm_g):
    bsz, s, _ = p3.shape
    nh = N_GROUP_HEADS
    hps = MLSTM_HEADS_PER_STEP
    nsteps = nh // hps
    w = hps * HEAD_DIM

    def col(group):
        return pl.BlockSpec((None, s, w), lambda b, h: (b, 0, group * nsteps + h))

    gb_row = jnp.zeros((1, LANES), F32).at[0, :2 * nh].set(gate_b)
    return pl.pallas_call(
        _mlstm_kernel,
        grid=(bsz, nsteps),
        in_specs=[pl.BlockSpec(memory_space=pltpu.SMEM),
                  col(0), col(1), col(2), col(3),
                  pl.BlockSpec((CONV_WIDTH, w), lambda b, h: (0, h)),
                  pl.BlockSpec((CONV_WIDTH, w), lambda b, h: (0, nsteps + h)),
                  pl.BlockSpec((None, s, LANES), lambda b, h: (b, 0, 0)),
                  pl.BlockSpec((1, LANES), lambda b, h: (0, 0)),
                  pl.BlockSpec((None, s // CHUNK, 2 * nh, CHUNK), lambda b, h: (b, 0, 0, 0)),
                  pl.BlockSpec((hps, 1, HEAD_DIM), lambda b, h: (h, 0, 0))],
        out_specs=pl.BlockSpec((None, s, w), lambda b, h: (b, 0, h)),
        out_shape=jax.ShapeDtypeStruct((bsz, s, GROUP_WIDTH), BF16),
        scratch_shapes=[pltpu.VMEM((s, w), BF16), pltpu.VMEM((s, w), BF16),
                        pltpu.VMEM((hps, HEAD_DIM, HEAD_DIM), F32), pltpu.VMEM((hps, 1, HEAD_DIM), F32),
                        pltpu.VMEM((hps, 1, 1), F32)],
        compiler_params=_params("arbitrary", "arbitrary"),
        name="mlstm_heads",
    )(gate_b, p3, p3, p3, p3, conv_w, conv_w, gates_col, gb_row, gates_row,
      head_norm_g.reshape(2 * nh, 1, HEAD_DIM))


def _ret_kernel(q_ref, k_ref, v_ref, g_ref, cos_ref, sin_ref, ng_ref, y_ref, q_s, k_s, r_s):
    hps = ng_ref.shape[0]
    h0 = pl.program_id(1) * hps
    s = q_ref.shape[0]
    half = HEAD_DIM // 2
    cos = cos_ref[...]
    sin = sin_ref[...]

    def rotate(x_ref, dst, scale):
        for j in range(hps):
            lo = slice(j * HEAD_DIM, j * HEAD_DIM + half)
            hi = slice(j * HEAD_DIM + half, (j + 1) * HEAD_DIM)
            x1 = x_ref[:, lo].astype(F32)
            x2 = x_ref[:, hi].astype(F32)
            dst[:, lo] = ((x1 * cos - x2 * sin) * scale).astype(BF16)
            dst[:, hi] = ((x2 * cos + x1 * sin) * scale).astype(BF16)

    rotate(q_ref, q_s, 1.0)
    rotate(k_ref, k_s, HEAD_DIM ** -0.5)
    r_s[...] = jnp.zeros_like(r_s)

    ri = lax.broadcasted_iota(I32, (CHUNK, CHUNK), 0)
    ci = lax.broadcasted_iota(I32, (CHUNK, CHUNK), 1)
    rel = (ri - ci).astype(F32)
    jcol = lax.broadcasted_iota(I32, (CHUNK, 1), 0).astype(F32)
    decays = []
    for j in range(hps):
        log_gamma = jnp.full((1, 1), math.log(1.0 - 2.0 ** -5.0), F32)
        for i in range(1, N_GROUP_HEADS):
            log_gamma = jnp.where(h0 + j == i, math.log(1.0 - 2.0 ** (-5.0 - i)), log_gamma)
        decays.append((jnp.where(rel >= 0, jnp.exp(jnp.maximum(rel, 0.0) * log_gamma), 0.0),
                       jnp.exp((jcol + 1.0) * log_gamma),
                       jnp.exp((CHUNK - 1.0 - jcol) * log_gamma),
                       jnp.exp(CHUNK * log_gamma)))

    def body(i, carry):
        for u in range(RET_CHUNK_UNROLL):
            st = pl.multiple_of((i * RET_CHUNK_UNROLL + u) * CHUNK, CHUNK)
            for j in range(hps):
                dmask, cross_decay, state_decay, chunk_decay = decays[j]
                cols = slice(j * HEAD_DIM, (j + 1) * HEAD_DIM)
                qc = q_s[pl.ds(st, CHUNK), cols]
                kc = k_s[pl.ds(st, CHUNK), cols]
                vc = v_ref[pl.ds(st, CHUNK), cols]
                inner = _dot((_dot_nt(qc, kc) * dmask).astype(BF16), vc)
                cross = _dot(qc, r_s[j].astype(BF16)) * cross_decay
                r_s[j] = chunk_decay * r_s[j] + _dot_tn((kc.astype(F32) * state_decay).astype(BF16), vc)
                g = g_ref[pl.ds(st, CHUNK), cols].astype(F32)
                y_ref[pl.ds(st, CHUNK), cols] = _group_norm_gate(inner + cross, ng_ref[j], g * _sigmoid(g))
        return carry

    lax.fori_loop(0, s // (CHUNK * RET_CHUNK_UNROLL), body, 0)


def _retention_heads(p3, cos, sin, head_norm_g):
    bsz, s, _ = p3.shape
    nh = N_GROUP_HEADS
    hps = RET_HEADS_PER_STEP
    nsteps = nh // hps
    w = hps * HEAD_DIM

    def col(group):
        return pl.BlockSpec((None, s, w), lambda b, h: (b, 0, group * nsteps + h))

    return pl.pallas_call(
        _ret_kernel,
        grid=(bsz, nsteps),
        in_specs=[col(4), col(5), col(6), col(7),
                  pl.BlockSpec((s, HEAD_DIM // 2), lambda b, h: (0, 0)),
                  pl.BlockSpec((s, HEAD_DIM // 2), lambda b, h: (0, 0)),
                  pl.BlockSpec((hps, 1, HEAD_DIM), lambda b, h: (nsteps + h, 0, 0))],
        out_specs=pl.BlockSpec((None, s, w), lambda b, h: (b, 0, h)),
        out_shape=jax.ShapeDtypeStruct((bsz, s, GROUP_WIDTH), BF16),
        scratch_shapes=[pltpu.VMEM((s, w), BF16), pltpu.VMEM((s, w), BF16),
                        pltpu.VMEM((hps, HEAD_DIM, HEAD_DIM), F32)],
        compiler_params=_params("arbitrary", "arbitrary"),
        name="retention_heads",
    )(p3, p3, p3, p3, cos, sin, head_norm_g.reshape(2 * nh, 1, HEAD_DIM))


def _order_key(x):
    bits = pltpu.bitcast(x, I32)
    return bits ^ ((bits >> 31) & 0x7FFFFFFF)


def _dsa_kernel(tab_ref, q_ref, iq_ref, iw_ref, k_ref, v_ref, ika_ref, ikb_ref, qg_ref, kg_ref, o_ref,
                kext_s, ikab_s, vaug_s, keyt_s, tb_s, wb_s, qall_s, iqall_s, acc_s, m_s, *, topk):
    b = pl.program_id(0)
    qq = pl.program_id(1)
    s = k_ref.shape[0]
    nkb = s // Q_BLOCK
    subs = range(DSA_SUBS)
    ri = lax.broadcasted_iota(I32, (Q_BLOCK, Q_BLOCK), 0)
    ci = lax.broadcasted_iota(I32, (Q_BLOCK, Q_BLOCK), 1)

    def qblock(sub):
        return DSA_SUBS * qq + sub

    @pl.when((b == 0) & (qq == 0))
    def _():
        buckets = []
        for back in range(2):
            dist = jnp.maximum(back * Q_BLOCK + ri - ci, 0)
            bucket = jnp.full(dist.shape, REL_BUCKETS // 2, I32)
            for thr in T5_THRESHOLDS:
                bucket = bucket + jnp.where(dist >= thr, 1, 0)
            buckets.append(jnp.where(dist < REL_BUCKETS // 2, dist, bucket))

        def head_body(hh, carry):
            far = tab_ref[REL_BUCKETS - 1, hh]
            tiles = []
            for bucket in buckets:
                tile = jnp.zeros(bucket.shape, F32)
                for bk in range(REL_BUCKETS):
                    tile = jnp.where(bucket == bk, (tab_ref[bk, hh] - far) * LOG2E, tile)
                tiles.append(tile)
            zeros = jnp.zeros((Q_BLOCK, Q_BLOCK), F32)
            for i, (first, second) in enumerate(((tiles[0], zeros), (tiles[1], tiles[0]), (zeros, tiles[1]))):
                tb_s[i, hh, :, :Q_BLOCK] = first
                tb_s[i, hh, :, Q_BLOCK:] = second
            return carry

        lax.fori_loop(0, N_ATTN_HEADS, head_body, 0)
        onehot = jnp.where(ri == ci, 1.0, 0.0).astype(BF16)
        for sub in subs:
            for hh in range(N_ATTN_HEADS):
                qall_s[sub, hh * Q_BLOCK:(hh + 1) * Q_BLOCK, ATTN_DIM:] = onehot

    @pl.when(qq == 0)
    def _():
        k = k_ref[...].astype(F32)
        r = lax.rsqrt(jnp.mean(k * k, axis=-1, keepdims=True) + EPS)
        kn = (k * r * kg_ref[...]).astype(BF16)
        for sub in subs:
            kext_s[sub, :, :ATTN_DIM] = kn
        for kt in range(nkb):
            ikab_s[kt, :Q_BLOCK, :] = ika_ref[kt * Q_BLOCK:(kt + 1) * Q_BLOCK, :]
            ikab_s[kt, Q_BLOCK:, :] = ikb_ref[kt * Q_BLOCK:(kt + 1) * Q_BLOCK, :]
        vaug_s[:, :ATTN_DIM] = v_ref[...]
        vaug_s[:, ATTN_DIM:] = jnp.ones((s, ATTN_DIM), BF16)

    ones = jnp.ones((ATTN_DIM, ATTN_DIM), BF16)
    pair_rows = N_IDX_HEADS // 2 * Q_BLOCK
    for sub in subs:
        rows = slice(sub * Q_BLOCK, (sub + 1) * Q_BLOCK)
        qf = jnp.concatenate([q_ref[rows, hh * ATTN_DIM:(hh + 1) * ATTN_DIM] for hh in range(N_ATTN_HEADS)],
                             axis=0).astype(F32)
        sq = qf * qf
        sq_hi = sq.astype(BF16)
        sq_lo = (sq - sq_hi.astype(F32)).astype(BF16)
        mean_sq = (_dot(sq_hi, ones) + _dot(sq_lo, ones)) * (1.0 / ATTN_DIM)
        qall_s[sub, :, :ATTN_DIM] = (qf * lax.rsqrt(mean_sq + EPS)
                                     * (qg_ref[...] * (ATTN_DIM ** -0.5 * LOG2E))).astype(BF16)
        for hp in range(N_IDX_HEADS // 2):
            iqall_s[sub * pair_rows + hp * Q_BLOCK:sub * pair_rows + (hp + 1) * Q_BLOCK, :] = (
                iq_ref[rows, hp * LANES:(hp + 1) * LANES])
        wv = iw_ref[rows, :] * ((N_IDX_HEADS ** -0.5) * (IDX_DIM ** -0.5))
        for hh in range(N_IDX_HEADS):
            wb_s[sub, hh] = jnp.broadcast_to(wv[:, hh:hh + 1], (Q_BLOCK, Q_BLOCK))

    n_tiles = qq + 1

    def score_tile(kt):
        for half in range(KEY_TILE // Q_BLOCK):
            kb = 2 * kt + half
            r = _dot_nt(iqall_s[...], ikab_s[kb])
            for sub in subs:
                acc = jnp.zeros((Q_BLOCK, Q_BLOCK), F32)
                for hp in range(N_IDX_HEADS // 2):
                    blk = r[sub * pair_rows + hp * Q_BLOCK:sub * pair_rows + (hp + 1) * Q_BLOCK]
                    acc = acc + jnp.maximum(blk[:, :Q_BLOCK], 0.0) * wb_s[sub, 2 * hp]
                    acc = acc + jnp.maximum(blk[:, Q_BLOCK:], 0.0) * wb_s[sub, 2 * hp + 1]
                causal_t = (kb * Q_BLOCK + ri) <= (qblock(sub) * Q_BLOCK + ci)
                keyt_s[sub, kt, half * Q_BLOCK:(half + 1) * Q_BLOCK, :] = jnp.where(
                    causal_t, _order_key(acc.T), INT_MIN)

    def score_pair(i, carry):
        score_tile(2 * i)
        score_tile(2 * i + 1)
        return carry

    lax.fori_loop(0, n_tiles // 2, score_pair, 0)

    @pl.when(n_tiles % 2 == 1)
    def _():
        score_tile(n_tiles - 1)

    def count(pred):
        def inner(kt, accs):
            out = []
            for sub in subs:
                hit = jnp.where(pred(keyt_s[sub, kt], sub), 1.0, 0.0)
                out.append(accs[sub] + jnp.sum(hit.reshape(KEY_TILE // COUNT_ROWS, COUNT_ROWS, Q_BLOCK), axis=0))
            return tuple(out)
        accs = lax.fori_loop(0, n_tiles, inner, tuple(jnp.zeros((COUNT_ROWS, Q_BLOCK), F32) for _ in subs))
        return tuple(jnp.sum(acc, axis=0, keepdims=True) for acc in accs)

    def count_ge(cands):
        return count(lambda keys, sub: keys >= cands[sub])

    zero = jnp.zeros((1, Q_BLOCK), I32)
    thr0 = tuple(jnp.where(n >= topk, zero, INT_MIN) for n in count_ge((zero,) * DSA_SUBS))

    def bit_body(i, thrs):
        cands = tuple(t + jnp.left_shift(jnp.int32(1), 30 - i) for t in thrs)
        counts = count_ge(cands)
        return tuple(jnp.where(counts[sub] >= topk, cands[sub], thrs[sub]) for sub in subs)

    thrs = lax.fori_loop(0, 31, bit_body, thr0)
    thrs = tuple(jnp.maximum(t, INT_MIN + 1) for t in thrs)

    tied_over = sum(jnp.where(n > topk, 1.0, 0.0) for n in count_ge(thrs))

    @pl.when(jnp.max(tied_over) > 0.0)
    def _():
        greater = count(lambda keys, sub: keys > thrs[sub])
        kr = lax.broadcasted_iota(I32, (KEY_TILE, KEY_TILE), 0)
        kc = lax.broadcasted_iota(I32, (KEY_TILE, KEY_TILE), 1)
        lower = jnp.where(kc <= kr, 1.0, 0.0).astype(BF16)

        def retire(kt, seen):
            out = []
            for sub in subs:
                keys = keyt_s[sub, kt]
                tied = keys == thrs[sub]
                prefix = _dot(lower, jnp.where(tied, 1.0, 0.0).astype(BF16)) + seen[sub]
                keyt_s[sub, kt] = jnp.where(tied & (prefix > topk - greater[sub]), INT_MIN, keys)
                out.append(prefix[KEY_TILE - 1:, :])
            return tuple(out)

        lax.fori_loop(0, n_tiles, retire, tuple(jnp.zeros((1, Q_BLOCK), F32) for _ in subs))

    acc_s[...] = jnp.zeros_like(acc_s)
    m_s[...] = jnp.full(m_s.shape, NEG_BIG, F32)

    def attn_tile(kt, biases):
        st = pl.multiple_of(kt * KEY_TILE, KEY_TILE)
        for sub in subs:
            kext_s[sub, pl.ds(st, KEY_TILE), ATTN_DIM:] = jnp.where(
                keyt_s[sub, kt] >= thrs[sub], 0.0, NEG_BIG).astype(BF16)
        for sub in subs:
            sc = _dot_nt(qall_s[sub], kext_s[sub, pl.ds(st, KEY_TILE), :])
            sc = sc.reshape(N_ATTN_HEADS, Q_BLOCK, KEY_TILE)
            if biases[sub] is not None:
                sc = sc + tb_s[biases[sub]]
            m_prev = m_s[sub]
            m_new = jnp.maximum(m_prev, jnp.max(sc, axis=-1, keepdims=True))
            alpha = jnp.exp2(m_prev - m_new).reshape(N_ATTN_HEADS * Q_BLOCK, ATTN_DIM)
            p = jnp.concatenate([jnp.exp2(sc[:, :, i * ATTN_DIM:(i + 1) * ATTN_DIM] - m_new)
                                 for i in range(KEY_TILE // ATTN_DIM)], axis=-1)
            p = p.reshape(N_ATTN_HEADS * Q_BLOCK, KEY_TILE)
            pv = _dot(p.astype(BF16), vaug_s[pl.ds(st, KEY_TILE), :])
            acc_s[sub, :, :ATTN_DIM] = alpha * acc_s[sub, :, :ATTN_DIM] + pv[:, :ATTN_DIM]
            acc_s[sub, :, ATTN_DIM:] = alpha * acc_s[sub, :, ATTN_DIM:] + pv[:, ATTN_DIM:]
            m_s[sub] = m_new

    def far_pair(i, carry):
        attn_tile(2 * i, (None,) * DSA_SUBS)
        attn_tile(2 * i + 1, (None,) * DSA_SUBS)
        return carry

    n_far = jnp.maximum(qq - 1, 0)
    lax.fori_loop(0, n_far // 2, far_pair, 0)

    @pl.when(n_far % 2 == 1)
    def _():
        attn_tile(n_far - 1, (None,) * DSA_SUBS)

    @pl.when(qq >= 1)
    def _():
        attn_tile(qq - 1, (2, None))
        attn_tile(qq, (0, 1))

    @pl.when(qq == 0)
    def _():
        attn_tile(0, (0, 1))

    for sub in subs:
        out = acc_s[sub, :, :ATTN_DIM] / acc_s[sub, :, ATTN_DIM:]
        for hh in range(N_ATTN_HEADS):
            o_ref[sub * Q_BLOCK:(sub + 1) * Q_BLOCK, hh * ATTN_DIM:(hh + 1) * ATTN_DIM] = (
                out[hh * Q_BLOCK:(hh + 1) * Q_BLOCK].astype(BF16))


def _dsa_attention(p3, iw3, q_norm_g, k_norm_g, rel_bias):
    bsz, s, _ = p3.shape
    nq = s // Q_BLOCK
    qw = N_ATTN_HEADS * ATTN_DIM
    iqw = N_IDX_HEADS * IDX_DIM
    small0 = (qw + iqw) // LANES
    step_rows = DSA_SUBS * Q_BLOCK

    def small(off):
        return pl.BlockSpec((None, s, LANES), lambda b, i: (b, 0, small0 + off))

    return pl.pallas_call(
        functools.partial(_dsa_kernel, topk=min(MAX_TOPK, s // 4)),
        grid=(bsz, nq // DSA_SUBS),
        in_specs=[pl.BlockSpec(memory_space=pltpu.SMEM),
                  pl.BlockSpec((None, step_rows, qw), lambda b, i: (b, i, 0)),
                  pl.BlockSpec((None, step_rows, iqw), lambda b, i: (b, i, qw // iqw)),
                  pl.BlockSpec((None, step_rows, LANES), lambda b, i: (b, i, 0)),
                  small(0), small(1), small(2), small(3),
                  pl.BlockSpec((1, ATTN_DIM), lambda b, i: (0, 0)),
                  pl.BlockSpec((1, ATTN_DIM), lambda b, i: (0, 0))],
        out_specs=pl.BlockSpec((None, step_rows, qw), lambda b, i: (b, i, 0)),
        out_shape=jax.ShapeDtypeStruct((bsz, s, qw), BF16),
        scratch_shapes=[pltpu.VMEM((DSA_SUBS, s, 2 * ATTN_DIM), BF16),
                        pltpu.VMEM((nq, 2 * Q_BLOCK, LANES), BF16),
                        pltpu.VMEM((s, 2 * ATTN_DIM), BF16),
                        pltpu.VMEM((DSA_SUBS, s // KEY_TILE, KEY_TILE, Q_BLOCK), I32),
                        pltpu.VMEM((3, N_ATTN_HEADS, Q_BLOCK, KEY_TILE), F32),
                        pltpu.VMEM((DSA_SUBS, N_IDX_HEADS, Q_BLOCK, Q_BLOCK), F32),
                        pltpu.VMEM((DSA_SUBS, N_ATTN_HEADS * Q_BLOCK, 2 * ATTN_DIM), BF16),
                        pltpu.VMEM((DSA_SUBS * N_IDX_HEADS // 2 * Q_BLOCK, LANES), BF16),
                        pltpu.VMEM((DSA_SUBS, N_ATTN_HEADS * Q_BLOCK, 2 * ATTN_DIM), F32),
                        pltpu.VMEM((DSA_SUBS, N_ATTN_HEADS, Q_BLOCK, ATTN_DIM), F32)],
        compiler_params=_params("arbitrary", "arbitrary"),
        name="dsa_attention",
    )(rel_bias, p3, p3, iw3, p3, p3, p3, p3, q_norm_g.reshape(1, ATTN_DIM), k_norm_g.reshape(1, ATTN_DIM))


def _outproj_kernel(ya_ref, yb_ref, x_ref, w_ref, g1_ref, n2_ref, sc2_ref, sh2_ref, x1_ref, h2_ref):
    half = ya_ref.shape[1]
    mix = _dot(ya_ref[...], w_ref[:half, :]) + _dot(yb_ref[...], w_ref[half:, :])
    x1 = x_ref[...] + g1_ref[...] * mix
    x1_ref[...] = x1
    h2_ref[...] = _norm_modulate(x1, n2_ref[...], sc2_ref[...], sh2_ref[...]).astype(BF16)


def _out_projection(ya, yb, yb_col, x2d, w_out, norm2_g, mod3, layer, bsz, tm):
    m, d = x2d.shape
    tpb = (m // bsz) // tm
    row = pl.BlockSpec((tm, d), lambda i: (i, 0))
    return pl.pallas_call(
        _outproj_kernel,
        grid=(m // tm,),
        in_specs=[pl.BlockSpec((tm, d // 2), lambda i: (i, 0)),
                  pl.BlockSpec((tm, d // 2), lambda i: (i, yb_col)),
                  row,
                  pl.BlockSpec((d, d), lambda i: (0, 0)),
                  _mod_spec(layer, 2, bsz, tpb),
                  pl.BlockSpec((1, d), lambda i: (0, 0)),
                  _mod_spec(layer, 4, bsz, tpb),
                  _mod_spec(layer, 3, bsz, tpb)],
        out_specs=[row, row],
        out_shape=[jax.ShapeDtypeStruct((m, d), F32), jax.ShapeDtypeStruct((m, d), BF16)],
        compiler_params=_params("arbitrary"),
        name=f"out_projection_{layer}",
    )(ya, yb, x2d, w_out, mod3, norm2_g.reshape(1, d), mod3, mod3)


def _mlp_kernel(h_ref, x1_ref, w1_ref, w2_ref, g2_ref, o_ref):
    j = pl.program_id(1)

    @pl.when(j == 0)
    def _():
        o_ref[...] = jnp.zeros_like(o_ref)

    a = jnp.square(jnp.maximum(_dot(h_ref[...], w1_ref[...]), 0.0)).astype(BF16)
    o_ref[...] += _dot(a, w2_ref[...])

    @pl.when(j == pl.num_programs(1) - 1)
    def _():
        o_ref[...] = x1_ref[...] + g2_ref[...] * o_ref[...]


def _mlp(h2, x1, w1, w2, mod3, layer, bsz, tm, tf):
    m, d = x1.shape
    ff = w1.shape[-1]
    tpb = (m // bsz) // tm
    row = pl.BlockSpec((tm, d), lambda i, j: (i, 0))
    return pl.pallas_call(
        _mlp_kernel,
        grid=(m // tm, ff // tf),
        in_specs=[row, row,
                  pl.BlockSpec((None, d, tf), lambda i, j: (layer, 0, j)),
                  pl.BlockSpec((None, tf, d), lambda i, j: (layer, j, 0)),
                  _mod_spec(layer, 5, bsz, tpb)],
        out_specs=row,
        out_shape=jax.ShapeDtypeStruct((m, d), F32),
        compiler_params=_params("arbitrary", "arbitrary"),
        name=f"mlp_{layer}",
    )(h2, x1, w1, w2, mod3)


def _pad_cols(w, width):
    return jnp.pad(w, ((0, 0), (0, width - w.shape[1])))


def _even_weights(w_in):
    gw = GROUP_WIDTH
    g0 = 4 * gw
    g1 = g0 + 2 * N_GROUP_HEADS
    main = jnp.concatenate([w_in[:, :g0], w_in[:, g1:]], axis=1).astype(BF16)
    tail = _pad_cols(w_in[:, g0:g1], LANES).astype(BF16)
    return main, tail


def _odd_weights(w_in):
    qw = N_ATTN_HEADS * ATTN_DIM
    iqw = N_IDX_HEADS * IDX_DIM
    o_k, o_v, o_iq = qw, qw + ATTN_DIM, qw + 2 * ATTN_DIM
    o_ik = o_iq + iqw
    o_iw = o_ik + IDX_DIM
    w_ik = w_in[:, o_ik:o_iw]
    zeros = jnp.zeros_like(w_ik)
    main = jnp.concatenate([w_in[:, :qw], w_in[:, o_iq:o_ik], w_in[:, o_k:o_v], w_in[:, o_v:o_iq],
                            w_ik, zeros, zeros, w_ik], axis=1).astype(BF16)
    tail = _pad_cols(w_in[:, o_iw:], LANES).astype(BF16)
    return main, tail


def kernel(x, c, ada_w, ada_b, norm1_g, norm2_g, mlp_w1, mlp_w2, even_w_in, even_conv_w, even_gate_b,
           even_head_norm_g, even_w_out, odd_w_in, odd_q_norm_g, odd_k_norm_g, odd_w_out, rel_bias):
    bsz, s, d = x.shape
    depth = ada_w.shape[0]
    m = bsz * s
    tm = ROW_TILE
    mod3 = _ada_modulation(c, ada_w, ada_b).reshape(depth * bsz * 6, 1, d)
    cos, sin = _rope_tables(s)
    w1_all = mlp_w1.astype(BF16)
    w2_all = mlp_w2.astype(BF16)
    xc = x.reshape(m, d)
    for l in range(depth):
        e = l // 2
        if l % 2 == 0:
            w_main, w_tail = _even_weights(even_w_in[e])
            p, gates = _in_projection(xc, norm1_g[l], mod3, l, bsz, w_main, w_tail, PROJ_ROW_TILE, EVEN_COL_TILE)
            p3 = p.reshape(bsz, s, -1)
            gates_col = gates.reshape(bsz, s, LANES)
            gates_row = jnp.swapaxes(
                gates_col[:, :, :2 * N_GROUP_HEADS].reshape(bsz, s // CHUNK, CHUNK, 2 * N_GROUP_HEADS), 2, 3)
            ya = _mlstm_heads(p3, gates_col, gates_row, even_conv_w[e], even_gate_b[e], even_head_norm_g[e])
            yb = _retention_heads(p3, cos, sin, even_head_norm_g[e])
            ya, yb, yb_col = ya.reshape(m, d // 2), yb.reshape(m, d // 2), 0
            w_out = even_w_out[e]
        else:
            w_main, w_tail = _odd_weights(odd_w_in[e])
            p, iw = _in_projection(xc, norm1_g[l], mod3, l, bsz, w_main, w_tail, PROJ_ROW_TILE, ODD_COL_TILE)
            y = _dsa_attention(p.reshape(bsz, s, -1), iw.reshape(bsz, s, LANES),
                               odd_q_norm_g[e], odd_k_norm_g[e], rel_bias)
            ya = yb = y.reshape(m, d)
            yb_col = 1
            w_out = odd_w_out[e]
        x1, h2 = _out_projection(ya, yb, yb_col, xc, w_out.astype(BF16), norm2_g[l], mod3, l, bsz, tm)
        xc = _mlp(h2, x1, w1_all, w2_all, mod3, l, bsz, tm, FF_TILE)
    return xc.reshape(bsz, s, d)
```

```python
import functools
import math

import numpy as np
import jax
import jax.numpy as jnp
from jax import lax
from jax.experimental import pallas as pl
from jax.experimental.pallas import tpu as pltpu

F32 = jnp.float32
BF16 = jnp.bfloat16
I32 = jnp.int32

D_MODEL = 2048
D_FF = 4 * D_MODEL
EPS = 1e-6
CHUNK = 128
HEAD_DIM = 256
N_GROUP_HEADS = 4
MLSTM_HEADS_PER_STEP = 2
RET_HEADS_PER_STEP = 4
RET_CHUNK_UNROLL = 4
GROUP_WIDTH = N_GROUP_HEADS * HEAD_DIM
CONV_WIDTH = 4
GATE_SOFTCAP = 15.0
ROPE_BASE = 10000.0
N_ATTN_HEADS = 16
ATTN_DIM = 128
N_IDX_HEADS = 16
IDX_DIM = 64
MAX_TOPK = 256
Q_BLOCK = 128
KEY_TILE = 2 * Q_BLOCK
DSA_SUBS = 2
assert DSA_SUBS * Q_BLOCK == KEY_TILE
REL_BUCKETS = 32
REL_MAX_DISTANCE = 128
LANES = 128
COUNT_ROWS = 64
INT_MIN = -(2 ** 31)
NEG_BIG = -1e30
LOG2E = math.log2(math.e)
VMEM_LIMIT = 56 * 1024 * 1024
ROW_TILE = 512
PROJ_ROW_TILE = 1024
FF_TILE = 1024
EVEN_COL_TILE = 2048
ODD_COL_TILE = 1792


def _t5_large_bucket_thresholds():
    max_exact = REL_BUCKETS // 2
    d = np.arange(0, 4 * REL_MAX_DISTANCE)
    large = max_exact + (np.log(np.maximum(d, 1) / max_exact) / math.log(REL_MAX_DISTANCE / max_exact)
                         * (REL_BUCKETS - max_exact)).astype(np.int64)
    bucket = np.where(d < max_exact, d, np.minimum(large, REL_BUCKETS - 1))
    return [int(d[bucket >= b].min()) for b in range(max_exact + 1, REL_BUCKETS)]


T5_THRESHOLDS = _t5_large_bucket_thresholds()
assert T5_THRESHOLDS[-1] <= Q_BLOCK + 1


def _params(*semantics):
    return pltpu.CompilerParams(dimension_semantics=semantics, vmem_limit_bytes=VMEM_LIMIT)


def _dot(a, b):
    return jnp.dot(a, b, preferred_element_type=F32)


def _dot_nt(a, b):
    return lax.dot_general(a, b, (((1,), (1,)), ((), ())), preferred_element_type=F32)


def _dot_tn(a, b):
    return lax.dot_general(a, b, (((0,), (0,)), ((), ())), preferred_element_type=F32)


def _sigmoid(x):
    return 1.0 / (1.0 + jnp.exp(-x))


def _log_sigmoid(x):
    return jnp.minimum(x, 0.0) - jnp.log1p(jnp.exp(-jnp.abs(x)))


def _softcap(x):
    return GATE_SOFTCAP * jnp.tanh(x / GATE_SOFTCAP)


def _norm_modulate(x, g, sc, sh):
    r = lax.rsqrt(jnp.mean(x * x, axis=-1, keepdims=True) + EPS)
    return (x * r * g) * (1.0 + sc) + sh


def _ada_kernel(c_ref, w_ref, b_ref, o_ref):
    c = c_ref[...]
    cond = c * _sigmoid(c)
    c_hi = cond.astype(BF16)
    c_lo = (cond - c_hi.astype(F32)).astype(BF16)
    w = w_ref[...]
    w_hi = w.astype(BF16)
    w_lo = (w - w_hi.astype(F32)).astype(BF16)
    o_ref[...] = _dot(c_hi, w_hi) + (_dot(c_hi, w_lo) + _dot(c_lo, w_hi)) + b_ref[...]


def _ada_modulation(c, ada_w, ada_b):
    depth, d, n = ada_w.shape
    bsz = c.shape[0]
    tn = 1024
    return pl.pallas_call(
        _ada_kernel,
        grid=(depth, n // tn),
        in_specs=[pl.BlockSpec((bsz, d), lambda l, j: (0, 0)),
                  pl.BlockSpec((None, d, tn), lambda l, j: (l, 0, j)),
                  pl.BlockSpec((None, 1, tn), lambda l, j: (l, 0, j))],
        out_specs=pl.BlockSpec((None, bsz, tn), lambda l, j: (l, 0, j)),
        out_shape=jax.ShapeDtypeStruct((depth, bsz, n), F32),
        compiler_params=_params("arbitrary", "arbitrary"),
        name="ada_modulation",
    )(c, ada_w, ada_b.reshape(depth, 1, n))


def _mod_spec(layer, which, bsz, tiles_per_batch):
    return pl.BlockSpec((None, 1, D_MODEL),
                        lambda i, *_: ((layer * bsz + i // tiles_per_batch) * 6 + which, 0, 0))


def _proj_kernel(x_ref, g_ref, sc_ref, sh_ref, w_ref, wt_ref, p_ref, t_ref, h_s):
    @pl.when(pl.program_id(1) == 0)
    def _():
        h = _norm_modulate(x_ref[...], g_ref[...], sc_ref[...], sh_ref[...]).astype(BF16)
        h_s[...] = h
        t_ref[...] = _dot(h, wt_ref[...])

    p_ref[...] = _dot(h_s[...], w_ref[...]).astype(p_ref.dtype)


def _in_projection(x2d, norm_g, mod3, layer, bsz, w_main, w_tail, tm, tn):
    m, d = x2d.shape
    n = w_main.shape[1]
    tpb = (m // bsz) // tm
    return pl.pallas_call(
        _proj_kernel,
        grid=(m // tm, n // tn),
        in_specs=[pl.BlockSpec((tm, d), lambda i, j: (i, 0)),
                  pl.BlockSpec((1, d), lambda i, j: (0, 0)),
                  _mod_spec(layer, 1, bsz, tpb),
                  _mod_spec(layer, 0, bsz, tpb),
                  pl.BlockSpec((d, tn), lambda i, j: (0, j)),
                  pl.BlockSpec((d, LANES), lambda i, j: (0, 0))],
        out_specs=[pl.BlockSpec((tm, tn), lambda i, j: (i, j)),
                   pl.BlockSpec((tm, LANES), lambda i, j: (i, 0))],
        out_shape=[jax.ShapeDtypeStruct((m, n), BF16),
                   jax.ShapeDtypeStruct((m, LANES), F32)],
        scratch_shapes=[pltpu.VMEM((tm, d), BF16)],
        compiler_params=_params("arbitrary", "arbitrary"),
        name=f"in_projection_{layer}",
    )(x2d, norm_g.reshape(1, d), mod3, mod3, w_main, w_tail)


def _rope_kernel(cos_ref, sin_ref):
    s, half = cos_ref.shape
    pos = lax.broadcasted_iota(I32, (s, half), 0).astype(F32)
    idx = lax.broadcasted_iota(I32, (s, half), 1).astype(F32)
    inv = jnp.exp(idx * (-math.log(ROPE_BASE) / half))
    ang = pos * inv
    cos_ref[...] = jnp.cos(ang)
    sin_ref[...] = jnp.sin(ang)


def _rope_tables(s):
    half = HEAD_DIM // 2
    return pl.pallas_call(
        _rope_kernel,
        out_shape=[jax.ShapeDtypeStruct((s, half), F32)] * 2,
        name="rope_tables",
    )()


def _group_norm_gate(h, ng, gate):
    r = lax.rsqrt(jnp.mean(h * h, axis=-1, keepdims=True) + EPS)
    return (h * r * ng * gate).astype(BF16)


def _mlstm_kernel(gb_ref, q_ref, k_ref, v_ref, o_ref, cwq_ref, cwk_ref, gc_ref, gbrow_ref, gr_ref,
                  ng_ref, y_ref, q_s, k_s, c_s, n_s, m_s):
    hps = ng_ref.shape[0]
    h0 = pl.program_id(1) * hps
    s = q_ref.shape[0]
    rows = lax.broadcasted_iota(I32, (s, 1), 0)

    def conv_silu(x_ref, cw_ref):
        x = x_ref[...].astype(F32)
        acc = x * cw_ref[CONV_WIDTH - 1:CONV_WIDTH, :]
        for sft in range(1, CONV_WIDTH):
            xs = jnp.where(rows >= sft, pltpu.roll(x, sft, 0), 0.0)
            acc = acc + xs * cw_ref[CONV_WIDTH - 1 - sft:CONV_WIDTH - sft, :]
        return acc * _sigmoid(acc)

    q_s[...] = conv_silu(q_ref, cwq_ref).astype(BF16)
    k_s[...] = (conv_silu(k_ref, cwk_ref) * (HEAD_DIM ** -0.5)).astype(BF16)
    c_s[...] = jnp.zeros_like(c_s)
    n_s[...] = jnp.zeros_like(n_s)
    m_s[...] = jnp.zeros_like(m_s)

    lane = lax.broadcasted_iota(I32, (CHUNK, LANES), 1)
    ri = lax.broadcasted_iota(I32, (CHUNK, CHUNK), 0)
    ci = lax.broadcasted_iota(I32, (CHUNK, CHUNK), 1)
    tril = ri >= ci

    def head_chunk(c, st, gcap, j):
        h = h0 + j
        cols = slice(j * HEAD_DIM, (j + 1) * HEAD_DIM)
        li_col = jnp.sum(jnp.where(lane == h, gcap, 0.0), axis=1, keepdims=True)
        lf_col = jnp.sum(jnp.where(lane == h + N_GROUP_HEADS, _log_sigmoid(gcap), 0.0), axis=1, keepdims=True)
        li_row = _softcap(gr_ref[c, pl.ds(h, 1), :] + gb_ref[h])
        lf_row = _log_sigmoid(_softcap(gr_ref[c, pl.ds(h + N_GROUP_HEADS, 1), :] + gb_ref[h + N_GROUP_HEADS]))
        bcum_col = jnp.sum(jnp.where(tril, lf_row, 0.0), axis=1, keepdims=True)
        bcum_row = jnp.sum(jnp.where(ri <= ci, lf_col, 0.0), axis=0, keepdims=True)
        b_last = jnp.sum(lf_row, axis=1, keepdims=True)
        m_prev = m_s[j]

        dmat = jnp.where(tril, bcum_col - bcum_row + li_row, -jnp.inf)
        m_inter = bcum_col + m_prev
        m_row = jnp.maximum(jnp.max(dmat, axis=1, keepdims=True), m_inter)
        qc = q_s[pl.ds(st, CHUNK), cols]
        kc = k_s[pl.ds(st, CHUNK), cols]
        vc = v_ref[pl.ds(st, CHUNK), cols]
        sc = _dot_nt(qc, kc) * jnp.exp(dmat - m_row)
        inter = jnp.exp(m_inter - m_row)
        num = _dot(sc.astype(BF16), vc) + inter * _dot(qc, c_s[j].astype(BF16))
        den = (jnp.sum(sc, axis=1, keepdims=True)
               + inter * jnp.sum(qc.astype(F32) * n_s[j], axis=1, keepdims=True))
        h_out = num / jnp.maximum(jnp.abs(den), jnp.exp(-m_row))

        g_col = b_last - bcum_col + li_col
        m_new = jnp.maximum(b_last + m_prev, jnp.max(g_col, axis=0, keepdims=True))
        decay = jnp.exp(b_last + m_prev - m_new)
        kw = kc.astype(F32) * jnp.exp(g_col - m_new)
        c_s[j] = decay * c_s[j] + _dot_tn(kw.astype(BF16), vc)
        n_s[j] = decay * n_s[j] + jnp.sum(kw, axis=0, keepdims=True)
        m_s[j] = m_new

        gate = _sigmoid(o_ref[pl.ds(st, CHUNK), cols].astype(F32))
        y_ref[pl.ds(st, CHUNK), cols] = _group_norm_gate(h_out, ng_ref[j], gate)

    def body(c, carry):
        st = pl.multiple_of(c * CHUNK, CHUNK)
        gcap = _softcap(gc_ref[pl.ds(st, CHUNK), :] + gbrow_ref[...])
        for j in range(hps):
            head_chunk(c, st, gcap, j)
        return carry

    lax.fori_loop(0, s // CHUNK, body, 0)


def _mlstm_heads(p3, gates_col, gates_row, conv_w, gate_b, head_norm_g):
    bsz, s, _ = p3.shape
    nh = N_GROUP_HEADS
    hps = MLSTM_HEADS_PER_STEP
    nsteps = nh // hps
    w = hps * HEAD_DIM

    def col(group):
        return pl.BlockSpec((None, s, w), lambda b, h: (b, 0, group * nsteps + h))

    gb_row = jnp.zeros((1, LANES), F32).at[0, :2 * nh].set(gate_b)
    return pl.pallas_call(
        _mlstm_kernel,
        grid=(bsz, nsteps),
        in_specs=[pl.BlockSpec(memory_space=pltpu.SMEM),
                  col(0), col(1), col(2), col(3),
                  pl.BlockSpec((CONV_WIDTH, w), lambda b, h: (0, h)),
                  pl.BlockSpec((CONV_WIDTH, w), lambda b, h: (0, nsteps + h)),
                  pl.BlockSpec((None, s, LANES), lambda b, h: (b, 0, 0)),
                  pl.BlockSpec((1, LANES), lambda b, h: (0, 0)),
                  pl.BlockSpec((None, s // CHUNK, 2 * nh, CHUNK), lambda b, h: (b, 0, 0, 0)),
                  pl.BlockSpec((hps, 1, HEAD_DIM), lambda b, h: (h, 0, 0))],
        out_specs=pl.BlockSpec((None, s, w), lambda b, h: (b, 0, h)),
        out_shape=jax.ShapeDtypeStruct((bsz, s, GROUP_WIDTH), BF16),
        scratch_shapes=[pltpu.VMEM((s, w), BF16), pltpu.VMEM((s, w), BF16),
                        pltpu.VMEM((hps, HEAD_DIM, HEAD_DIM), F32), pltpu.VMEM((hps, 1, HEAD_DIM), F32),
                        pltpu.VMEM((hps, 1, 1), F32)],
        compiler_params=_params("arbitrary", "arbitrary"),
        name="mlstm_heads",
    )(gate_b, p3, p3, p3, p3, conv_w, conv_w, gates_col, gb_row, gates_row,
      head_norm_g.reshape(2 * nh, 1, HEAD_DIM))


def _ret_kernel(q_ref, k_ref, v_ref, g_ref, cos_ref, sin_ref, ng_ref, y_ref, q_s, k_s, r_s):
    hps = ng_ref.shape[0]
    h0 = pl.program_id(1) * hps
    s = q_ref.shape[0]
    half = HEAD_DIM // 2
    cos = cos_ref[...]
    sin = sin_ref[...]

    def rotate(x_ref, dst, scale):
        for j in range(hps):
            lo = slice(j * HEAD_DIM, j * HEAD_DIM + half)
            hi = slice(j * HEAD_DIM + half, (j + 1) * HEAD_DIM)
            x1 = x_ref[:, lo].astype(F32)
            x2 = x_ref[:, hi].astype(F32)
            dst[:, lo] = ((x1 * cos - x2 * sin) * scale).astype(BF16)
            dst[:, hi] = ((x2 * cos + x1 * sin) * scale).astype(BF16)

    rotate(q_ref, q_s, 1.0)
    rotate(k_ref, k_s, HEAD_DIM ** -0.5)
    r_s[...] = jnp.zeros_like(r_s)

    ri = lax.broadcasted_iota(I32, (CHUNK, CHUNK), 0)
    ci = lax.broadcasted_iota(I32, (CHUNK, CHUNK), 1)
    rel = (ri - ci).astype(F32)
    jcol = lax.broadcasted_iota(I32, (CHUNK, 1), 0).astype(F32)
    decays = []
    for j in range(hps):
        log_gamma = jnp.full((1, 1), math.log(1.0 - 2.0 ** -5.0), F32)
        for i in range(1, N_GROUP_HEADS):
            log_gamma = jnp.where(h0 + j == i, math.log(1.0 - 2.0 ** (-5.0 - i)), log_gamma)
        decays.append((jnp.where(rel >= 0, jnp.exp(jnp.maximum(rel, 0.0) * log_gamma), 0.0),
                       jnp.exp((jcol + 1.0) * log_gamma),
                       jnp.exp((CHUNK - 1.0 - jcol) * log_gamma),
                       jnp.exp(CHUNK * log_gamma)))

    def body(i, carry):
        for u in range(RET_CHUNK_UNROLL):
            st = pl.multiple_of((i * RET_CHUNK_UNROLL + u) * CHUNK, CHUNK)
            for j in range(hps):
                dmask, cross_decay, state_decay, chunk_decay = decays[j]
                cols = slice(j * HEAD_DIM, (j + 1) * HEAD_DIM)
                qc = q_s[pl.ds(st, CHUNK), cols]
                kc = k_s[pl.ds(st, CHUNK), cols]
                vc = v_ref[pl.ds(st, CHUNK), cols]
                inner = _dot((_dot_nt(qc, kc) * dmask).astype(BF16), vc)
                cross = _dot(qc, r_s[j].astype(BF16)) * cross_decay
                r_s[j] = chunk_decay * r_s[j] + _dot_tn((kc.astype(F32) * state_decay).astype(BF16), vc)
                g = g_ref[pl.ds(st, CHUNK), cols].astype(F32)
                y_ref[pl.ds(st, CHUNK), cols] = _group_norm_gate(inner + cross, ng_ref[j], g * _sigmoid(g))
        return carry

    lax.fori_loop(0, s // (CHUNK * RET_CHUNK_UNROLL), body, 0)


def _retention_heads(p3, cos, sin, head_norm_g):
    bsz, s, _ = p3.shape
    nh = N_GROUP_HEADS
    hps = RET_HEADS_PER_STEP
    nsteps = nh // hps
    w = hps * HEAD_DIM

    def col(group):
        return pl.BlockSpec((None, s, w), lambda b, h: (b, 0, group * nsteps + h))

    return pl.pallas_call(
        _ret_kernel,
        grid=(bsz, nsteps),
        in_specs=[col(4), col(5), col(6), col(7),
                  pl.BlockSpec((s, HEAD_DIM // 2), lambda b, h: (0, 0)),
                  pl.BlockSpec((s, HEAD_DIM // 2), lambda b, h: (0, 0)),
                  pl.BlockSpec((hps, 1, HEAD_DIM), lambda b, h: (nsteps + h, 0, 0))],
        out_specs=pl.BlockSpec((None, s, w), lambda b, h: (b, 0, h)),
        out_shape=jax.ShapeDtypeStruct((bsz, s, GROUP_WIDTH), BF16),
        scratch_shapes=[pltpu.VMEM((s, w), BF16), pltpu.VMEM((s, w), BF16),
                        pltpu.VMEM((hps, HEAD_DIM, HEAD_DIM), F32)],
        compiler_params=_params("arbitrary", "arbitrary"),
        name="retention_heads",
    )(p3, p3, p3, p3, cos, sin, head_norm_g.reshape(2 * nh, 1, HEAD_DIM))


def _order_key(x):
    bits = pltpu.bitcast(x, I32)
    return bits ^ ((bits >> 31) & 0x7FFFFFFF)


def _dsa_kernel(tab_ref, q_ref, iq_ref, iw_ref, k_ref, v_ref, ika_ref, ikb_ref, qg_ref, kg_ref, o_ref,
                kext_s, ikab_s, vaug_s, keyt_s, tb_s, wb_s, qall_s, iqall_s, acc_s, m_s, *, topk):
    b = pl.program_id(0)
    qq = pl.program_id(1)
    s = k_ref.shape[0]
    nkb = s // Q_BLOCK
    subs = range(DSA_SUBS)
    ri = lax.broadcasted_iota(I32, (Q_BLOCK, Q_BLOCK), 0)
    ci = lax.broadcasted_iota(I32, (Q_BLOCK, Q_BLOCK), 1)

    def qblock(sub):
        return DSA_SUBS * qq + sub

    @pl.when((b == 0) & (qq == 0))
    def _():
        buckets = []
        for back in range(2):
            dist = jnp.maximum(back * Q_BLOCK + ri - ci, 0)
            bucket = jnp.full(dist.shape, REL_BUCKETS // 2, I32)
            for thr in T5_THRESHOLDS:
                bucket = bucket + jnp.where(dist >= thr, 1, 0)
            buckets.append(jnp.where(dist < REL_BUCKETS // 2, dist, bucket))

        def head_body(hh, carry):
            far = tab_ref[REL_BUCKETS - 1, hh]
            tiles = []
            for bucket in buckets:
                tile = jnp.zeros(bucket.shape, F32)
                for bk in range(REL_BUCKETS):
                    tile = jnp.where(bucket == bk, (tab_ref[bk, hh] - far) * LOG2E, tile)
                tiles.append(tile)
            zeros = jnp.zeros((Q_BLOCK, Q_BLOCK), F32)
            for i, (first, second) in enumerate(((tiles[0], zeros), (tiles[1], tiles[0]), (zeros, tiles[1]))):
                tb_s[i, hh, :, :Q_BLOCK] = first
                tb_s[i, hh, :, Q_BLOCK:] = second
            return carry

        lax.fori_loop(0, N_ATTN_HEADS, head_body, 0)
        onehot = jnp.where(ri == ci, 1.0, 0.0).astype(BF16)
        for sub in subs:
            for hh in range(N_ATTN_HEADS):
                qall_s[sub, hh * Q_BLOCK:(hh + 1) * Q_BLOCK, ATTN_DIM:] = onehot

    @pl.when(qq == 0)
    def _():
        k = k_ref[...].astype(F32)
        r = lax.rsqrt(jnp.mean(k * k, axis=-1, keepdims=True) + EPS)
        kn = (k * r * kg_ref[...]).astype(BF16)
        for sub in subs:
            kext_s[sub, :, :ATTN_DIM] = kn
        for kt in range(nkb):
            ikab_s[kt, :Q_BLOCK, :] = ika_ref[kt * Q_BLOCK:(kt + 1) * Q_BLOCK, :]
            ikab_s[kt, Q_BLOCK:, :] = ikb_ref[kt * Q_BLOCK:(kt + 1) * Q_BLOCK, :]
        vaug_s[:, :ATTN_DIM] = v_ref[...]
        vaug_s[:, ATTN_DIM:] = jnp.ones((s, ATTN_DIM), BF16)

    ones = jnp.ones((ATTN_DIM, ATTN_DIM), BF16)
    pair_rows = N_IDX_HEADS // 2 * Q_BLOCK
    for sub in subs:
        rows = slice(sub * Q_BLOCK, (sub + 1) * Q_BLOCK)
        qf = jnp.concatenate([q_ref[rows, hh * ATTN_DIM:(hh + 1) * ATTN_DIM] for hh in range(N_ATTN_HEADS)],
                             axis=0).astype(F32)
        sq = qf * qf
        sq_hi = sq.astype(BF16)
        sq_lo = (sq - sq_hi.astype(F32)).astype(BF16)
        mean_sq = (_dot(sq_hi, ones) + _dot(sq_lo, ones)) * (1.0 / ATTN_DIM)
        qall_s[sub, :, :ATTN_DIM] = (qf * lax.rsqrt(mean_sq + EPS)
                                     * (qg_ref[...] * (ATTN_DIM ** -0.5 * LOG2E))).astype(BF16)
        for hp in range(N_IDX_HEADS // 2):
            iqall_s[sub * pair_rows + hp * Q_BLOCK:sub * pair_rows + (hp + 1) * Q_BLOCK, :] = (
                iq_ref[rows, hp * LANES:(hp + 1) * LANES])
        wv = iw_ref[rows, :] * ((N_IDX_HEADS ** -0.5) * (IDX_DIM ** -0.5))
        for hh in range(N_IDX_HEADS):
            wb_s[sub, hh] = jnp.broadcast_to(wv[:, hh:hh + 1], (Q_BLOCK, Q_BLOCK))

    n_tiles = qq + 1

    def score_tile(kt):
        for half in range(KEY_TILE // Q_BLOCK):
            kb = 2 * kt + half
            r = _dot_nt(iqall_s[...], ikab_s[kb])
            for sub in subs:
                acc = jnp.zeros((Q_BLOCK, Q_BLOCK), F32)
                for hp in range(N_IDX_HEADS // 2):
                    blk = r[sub * pair_rows + hp * Q_BLOCK:sub * pair_rows + (hp + 1) * Q_BLOCK]
                    acc = acc + jnp.maximum(blk[:, :Q_BLOCK], 0.0) * wb_s[sub, 2 * hp]
                    acc = acc + jnp.maximum(blk[:, Q_BLOCK:], 0.0) * wb_s[sub, 2 * hp + 1]
                causal_t = (kb * Q_BLOCK + ri) <= (qblock(sub) * Q_BLOCK + ci)
                keyt_s[sub, kt, half * Q_BLOCK:(half + 1) * Q_BLOCK, :] = jnp.where(
                    causal_t, _order_key(acc.T), INT_MIN)

    def score_pair(i, carry):
        score_tile(2 * i)
        score_tile(2 * i + 1)
        return carry

    lax.fori_loop(0, n_tiles // 2, score_pair, 0)

    @pl.when(n_tiles % 2 == 1)
    def _():
        score_tile(n_tiles - 1)

    def count(pred):
        def inner(kt, accs):
            out = []
            for sub in subs:
                hit = jnp.where(pred(keyt_s[sub, kt], sub), 1.0, 0.0)
                out.append(accs[sub] + jnp.sum(hit.reshape(KEY_TILE // COUNT_ROWS, COUNT_ROWS, Q_BLOCK), axis=0))
            return tuple(out)
        accs = lax.fori_loop(0, n_tiles, inner, tuple(jnp.zeros((COUNT_ROWS, Q_BLOCK), F32) for _ in subs))
        return tuple(jnp.sum(acc, axis=0, keepdims=True) for acc in accs)

    def count_ge(cands):
        return count(lambda keys, sub: keys >= cands[sub])

    zero = jnp.zeros((1, Q_BLOCK), I32)
    thr0 = tuple(jnp.where(n >= topk, zero, INT_MIN) for n in count_ge((zero,) * DSA_SUBS))

    def bit_body(i, thrs):
        cands = tuple(t + jnp.left_shift(jnp.int32(1), 30 - i) for t in thrs)
        counts = count_ge(cands)
        return tuple(jnp.where(counts[sub] >= topk, cands[sub], thrs[sub]) for sub in subs)

    thrs = lax.fori_loop(0, 31, bit_body, thr0)
    thrs = tuple(jnp.maximum(t, INT_MIN + 1) for t in thrs)

    tied_over = sum(jnp.where(n > topk, 1.0, 0.0) for n in count_ge(thrs))

    @pl.when(jnp.max(tied_over) > 0.0)
    def _():
        greater = count(lambda keys, sub: keys > thrs[sub])
        kr = lax.broadcasted_iota(I32, (KEY_TILE, KEY_TILE), 0)
        kc = lax.broadcasted_iota(I32, (KEY_TILE, KEY_TILE), 1)
        lower = jnp.where(kc <= kr, 1.0, 0.0).astype(BF16)

        def retire(kt, seen):
            out = []
            for sub in subs:
                keys = keyt_s[sub, kt]
                tied = keys == thrs[sub]
                prefix = _dot(lower, jnp.where(tied, 1.0, 0.0).astype(BF16)) + seen[sub]
                keyt_s[sub, kt] = jnp.where(tied & (prefix > topk - greater[sub]), INT_MIN, keys)
                out.append(prefix[KEY_TILE - 1:, :])
            return tuple(out)

        lax.fori_loop(0, n_tiles, retire, tuple(jnp.zeros((1, Q_BLOCK), F32) for _ in subs))

    acc_s[...] = jnp.zeros_like(acc_s)
    m_s[...] = jnp.full(m_s.shape, NEG_BIG, F32)

    def attn_tile(kt, biases):
        st = pl.multiple_of(kt * KEY_TILE, KEY_TILE)
        for sub in subs:
            kext_s[sub, pl.ds(st, KEY_TILE), ATTN_DIM:] = jnp.where(
                keyt_s[sub, kt] >= thrs[sub], 0.0, NEG_BIG).astype(BF16)
        for sub in subs:
            sc = _dot_nt(qall_s[sub], kext_s[sub, pl.ds(st, KEY_TILE), :])
            sc = sc.reshape(N_ATTN_HEADS, Q_BLOCK, KEY_TILE)
            if biases[sub] is not None:
                sc = sc + tb_s[biases[sub]]
            m_prev = m_s[sub]
            m_new = jnp.maximum(m_prev, jnp.max(sc, axis=-1, keepdims=True))
            alpha = jnp.exp2(m_prev - m_new).reshape(N_ATTN_HEADS * Q_BLOCK, ATTN_DIM)
            p = jnp.concatenate([jnp.exp2(sc[:, :, i * ATTN_DIM:(i + 1) * ATTN_DIM] - m_new)
                                 for i in range(KEY_TILE // ATTN_DIM)], axis=-1)
            p = p.reshape(N_ATTN_HEADS * Q_BLOCK, KEY_TILE)
            pv = _dot(p.astype(BF16), vaug_s[pl.ds(st, KEY_TILE), :])
            acc_s[sub, :, :ATTN_DIM] = alpha * acc_s[sub, :, :ATTN_DIM] + pv[:, :ATTN_DIM]
            acc_s[sub, :, ATTN_DIM:] = alpha * acc_s[sub, :, ATTN_DIM:] + pv[:, ATTN_DIM:]
            m_s[sub] = m_new

    def far_pair(i, carry):
        attn_tile(2 * i, (None,) * DSA_SUBS)
        attn_tile(2 * i + 1, (None,) * DSA_SUBS)
        return carry

    n_far = jnp.maximum(qq - 1, 0)
    lax.fori_loop(0, n_far // 2, far_pair, 0)

    @pl.when(n_far % 2 == 1)
    def _():
        attn_tile(n_far - 1, (None,) * DSA_SUBS)

    @pl.when(qq >= 1)
    def _():
        attn_tile(qq - 1, (2, None))
        attn_tile(qq, (0, 1))

    @pl.when(qq == 0)
    def _():
        attn_tile(0, (0, 1))

    for sub in subs:
        out = acc_s[sub, :, :ATTN_DIM] / acc_s[sub, :, ATTN_DIM:]
        for hh in range(N_ATTN_HEADS):
            o_ref[sub * Q_BLOCK:(sub + 1) * Q_BLOCK, hh * ATTN_DIM:(hh + 1) * ATTN_DIM] = (
                out[hh * Q_BLOCK:(hh + 1) * Q_BLOCK].astype(BF16))


def _dsa_attention(p3, iw3, q_norm_g, k_norm_g, rel_bias):
    bsz, s, _ = p3.shape
    nq = s // Q_BLOCK
    qw = N_ATTN_HEADS * ATTN_DIM
    iqw = N_IDX_HEADS * IDX_DIM
    small0 = (qw + iqw) // LANES
    step_rows = DSA_SUBS * Q_BLOCK

    def small(off):
        return pl.BlockSpec((None, s, LANES), lambda b, i: (b, 0, small0 + off))

    return pl.pallas_call(
        functools.partial(_dsa_kernel, topk=min(MAX_TOPK, s // 4)),
        grid=(bsz, nq // DSA_SUBS),
        in_specs=[pl.BlockSpec(memory_space=pltpu.SMEM),
                  pl.BlockSpec((None, step_rows, qw), lambda b, i: (b, i, 0)),
                  pl.BlockSpec((None, step_rows, iqw), lambda b, i: (b, i, qw // iqw)),
                  pl.BlockSpec((None, step_rows, LANES), lambda b, i: (b, i, 0)),
                  small(0), small(1), small(2), small(3),
                  pl.BlockSpec((1, ATTN_DIM), lambda b, i: (0, 0)),
                  pl.BlockSpec((1, ATTN_DIM), lambda b, i: (0, 0))],
        out_specs=pl.BlockSpec((None, step_rows, qw), lambda b, i: (b, i, 0)),
        out_shape=jax.ShapeDtypeStruct((bsz, s, qw), BF16),
        scratch_shapes=[pltpu.VMEM((DSA_SUBS, s, 2 * ATTN_DIM), BF16),
                        pltpu.VMEM((nq, 2 * Q_BLOCK, LANES), BF16),
                        pltpu.VMEM((s, 2 * ATTN_DIM), BF16),
                        pltpu.VMEM((DSA_SUBS, s // KEY_TILE, KEY_TILE, Q_BLOCK), I32),
                        pltpu.VMEM((3, N_ATTN_HEADS, Q_BLOCK, KEY_TILE), F32),
                        pltpu.VMEM((DSA_SUBS, N_IDX_HEADS, Q_BLOCK, Q_BLOCK), F32),
                        pltpu.VMEM((DSA_SUBS, N_ATTN_HEADS * Q_BLOCK, 2 * ATTN_DIM), BF16),
                        pltpu.VMEM((DSA_SUBS * N_IDX_HEADS // 2 * Q_BLOCK, LANES), BF16),
                        pltpu.VMEM((DSA_SUBS, N_ATTN_HEADS * Q_BLOCK, 2 * ATTN_DIM), F32),
                        pltpu.VMEM((DSA_SUBS, N_ATTN_HEADS, Q_BLOCK, ATTN_DIM), F32)],
        compiler_params=_params("arbitrary", "arbitrary"),
        name="dsa_attention",
    )(rel_bias, p3, p3, iw3, p3, p3, p3, p3, q_norm_g.reshape(1, ATTN_DIM), k_norm_g.reshape(1, ATTN_DIM))


def _outproj_kernel(ya_ref, yb_ref, x_ref, w_ref, g1_ref, n2_ref, sc2_ref, sh2_ref, x1_ref, h2_ref):
    half = ya_ref.shape[1]
    mix = _dot(ya_ref[...], w_ref[:half, :]) + _dot(yb_ref[...], w_ref[half:, :])
    x1 = x_ref[...] + g1_ref[...] * mix
    x1_ref[...] = x1
    h2_ref[...] = _norm_modulate(x1, n2_ref[...], sc2_ref[...], sh2_ref[...]).astype(BF16)


def _out_projection(ya, yb, yb_col, x2d, w_out, norm2_g, mod3, layer, bsz, tm):
    m, d = x2d.shape
    tpb = (m // bsz) // tm
    row = pl.BlockSpec((tm, d), lambda i: (i, 0))
    return pl.pallas_call(
        _outproj_kernel,
        grid=(m // tm,),
        in_specs=[pl.BlockSpec((tm, d // 2), lambda i: (i, 0)),
                  pl.BlockSpec((tm, d // 2), lambda i: (i, yb_col)),
                  row,
                  pl.BlockSpec((d, d), lambda i: (0, 0)),
                  _mod_spec(layer, 2, bsz, tpb),
                  pl.BlockSpec((1, d), lambda i: (0, 0)),
                  _mod_spec(layer, 4, bsz, tpb),
                  _mod_spec(layer, 3, bsz, tpb)],
        out_specs=[row, row],
        out_shape=[jax.ShapeDtypeStruct((m, d), F32), jax.ShapeDtypeStruct((m, d), BF16)],
        compiler_params=_params("arbitrary"),
        name=f"out_projection_{layer}",
    )(ya, yb, x2d, w_out, mod3, norm2_g.reshape(1, d), mod3, mod3)


def _mlp_kernel(h_ref, x1_ref, w1_ref, w2_ref, g2_ref, o_ref):
    j = pl.program_id(1)

    @pl.when(j == 0)
    def _():
        o_ref[...] = jnp.zeros_like(o_ref)

    a = jnp.square(jnp.maximum(_dot(h_ref[...], w1_ref[...]), 0.0)).astype(BF16)
    o_ref[...] += _dot(a, w2_ref[...])

    @pl.when(j == pl.num_programs(1) - 1)
    def _():
        o_ref[...] = x1_ref[...] + g2_ref[...] * o_ref[...]


def _mlp(h2, x1, w1, w2, mod3, layer, bsz, tm, tf):
    m, d = x1.shape
    ff = w1.shape[-1]
    tpb = (m // bsz) // tm
    row = pl.BlockSpec((tm, d), lambda i, j: (i, 0))
    return pl.pallas_call(
        _mlp_kernel,
        grid=(m // tm, ff // tf),
        in_specs=[row, row,
                  pl.BlockSpec((None, d, tf), lambda i, j: (layer, 0, j)),
                  pl.BlockSpec((None, tf, d), lambda i, j: (layer, j, 0)),
                  _mod_spec(layer, 5, bsz, tpb)],
        out_specs=row,
        out_shape=jax.ShapeDtypeStruct((m, d), F32),
        compiler_params=_params("arbitrary", "arbitrary"),
        name=f"mlp_{layer}",
    )(h2, x1, w1, w2, mod3)


def _pad_cols(w, width):
    return jnp.pad(w, ((0, 0), (0, width - w.shape[1])))


def _even_weights(w_in):
    gw = GROUP_WIDTH
    g0 = 4 * gw
    g1 = g0 + 2 * N_GROUP_HEADS
    main = jnp.concatenate([w_in[:, :g0], w_in[:, g1:]], axis=1).astype(BF16)
    tail = _pad_cols(w_in[:, g0:g1], LANES).astype(BF16)
    return main, tail


def _odd_weights(w_in):
    qw = N_ATTN_HEADS * ATTN_DIM
    iqw = N_IDX_HEADS * IDX_DIM
    o_k, o_v, o_iq = qw, qw + ATTN_DIM, qw + 2 * ATTN_DIM
    o_ik = o_iq + iqw
    o_iw = o_ik + IDX_DIM
    w_ik = w_in[:, o_ik:o_iw]
    zeros = jnp.zeros_like(w_ik)
    main = jnp.concatenate([w_in[:, :qw], w_in[:, o_iq:o_ik], w_in[:, o_k:o_v], w_in[:, o_v:o_iq],
                            w_ik, zeros, zeros, w_ik], axis=1).astype(BF16)
    tail = _pad_cols(w_in[:, o_iw:], LANES).astype(BF16)
    return main, tail


def kernel(x, c, ada_w, ada_b, norm1_g, norm2_g, mlp_w1, mlp_w2, even_w_in, even_conv_w, even_gate_b,
           even_head_norm_g, even_w_out, odd_w_in, odd_q_norm_g, odd_k_norm_g, odd_w_out, rel_bias):
    bsz, s, d = x.shape
    depth = ada_w.shape[0]
    m = bsz * s
    tm = ROW_TILE
    mod3 = _ada_modulation(c, ada_w, ada_b).reshape(depth * bsz * 6, 1, d)
    cos, sin = _rope_tables(s)
    w1_all = mlp_w1.astype(BF16)
    w2_all = mlp_w2.astype(BF16)
    xc = x.reshape(m, d)
    for l in range(depth):
        e = l // 2
        if l % 2 == 0:
            w_main, w_tail = _even_weights(even_w_in[e])
            p, gates = _in_projection(xc, norm1_g[l], mod3, l, bsz, w_main, w_tail, PROJ_ROW_TILE, EVEN_COL_TILE)
            p3 = p.reshape(bsz, s, -1)
            gates_col = gates.reshape(bsz, s, LANES)
            gates_row = jnp.swapaxes(
                gates_col[:, :, :2 * N_GROUP_HEADS].reshape(bsz, s // CHUNK, CHUNK, 2 * N_GROUP_HEADS), 2, 3)
            ya = _mlstm_heads(p3, gates_col, gates_row, even_conv_w[e], even_gate_b[e], even_head_norm_g[e])
            yb = _retention_heads(p3, cos, sin, even_head_norm_g[e])
            ya, yb, yb_col = ya.reshape(m, d // 2), yb.reshape(m, d // 2), 0
            w_out = even_w_out[e]
        else:
            w_main, w_tail = _odd_weights(odd_w_in[e])
            p, iw = _in_projection(xc, norm1_g[l], mod3, l, bsz, w_main, w_tail, PROJ_ROW_TILE, ODD_COL_TILE)
            y = _dsa_attention(p.reshape(bsz, s, -1), iw.reshape(bsz, s, LANES),
                               odd_q_norm_g[e], odd_k_norm_g[e], rel_bias)
            ya = yb = y.reshape(m, d)
            yb_col = 1
            w_out = odd_w_out[e]
        x1, h2 = _out_projection(ya, yb, yb_col, xc, w_out.astype(BF16), norm2_g[l], mod3, l, bsz, tm)
        xc = _mlp(h2, x1, w1_all, w2_all, mod3, l, bsz, tm, FF_TILE)
    return xc.reshape(bsz, s, d)
```

```python
import functools
import math

import numpy as np
import jax
import jax.numpy as jnp
from jax import lax
from jax.experimental import pallas as pl
from jax.experimental.pallas import tpu as pltpu

F32 = jnp.float32
BF16 = jnp.bfloat16
I32 = jnp.int32

D_MODEL = 2048
D_FF = 4 * D_MODEL
EPS = 1e-6
CHUNK = 128
HEAD_DIM = 256
N_GROUP_HEADS = 4
MLSTM_HEADS_PER_STEP = 2
RET_HEADS_PER_STEP = 4
RET_CHUNK_UNROLL = 4
GROUP_WIDTH = N_GROUP_HEADS * HEAD_DIM
CONV_WIDTH = 4
GATE_SOFTCAP = 15.0
ROPE_BASE = 10000.0
N_ATTN_HEADS = 16
ATTN_DIM = 128
N_IDX_HEADS = 16
IDX_DIM = 64
MAX_TOPK = 256
Q_BLOCK = 128
KEY_TILE = 2 * Q_BLOCK
DSA_SUBS = 2
assert DSA_SUBS * Q_BLOCK == KEY_TILE
REL_BUCKETS = 32
REL_MAX_DISTANCE = 128
LANES = 128
COUNT_ROWS = 64
INT_MIN = -(2 ** 31)
NEG_BIG = -1e30
LOG2E = math.log2(math.e)
VMEM_LIMIT = 56 * 1024 * 1024
ROW_TILE = 512
MLP_ROW_TILE = 1024
PROJ_ROW_TILE = 1024
FF_TILE = 1024
EVEN_COL_TILE = 2048
ODD_COL_TILE = 1792


def _t5_large_bucket_thresholds():
    max_exact = REL_BUCKETS // 2
    d = np.arange(0, 4 * REL_MAX_DISTANCE)
    large = max_exact + (np.log(np.maximum(d, 1) / max_exact) / math.log(REL_MAX_DISTANCE / max_exact)
                         * (REL_BUCKETS - max_exact)).astype(np.int64)
    bucket = np.where(d < max_exact, d, np.minimum(large, REL_BUCKETS - 1))
    return [int(d[bucket >= b].min()) for b in range(max_exact + 1, REL_BUCKETS)]


T5_THRESHOLDS = _t5_large_bucket_thresholds()
assert T5_THRESHOLDS[-1] <= Q_BLOCK + 1


def _params(*semantics):
    return pltpu.CompilerParams(dimension_semantics=semantics, vmem_limit_bytes=VMEM_LIMIT)


def _dot(a, b):
    return jnp.dot(a, b, preferred_element_type=F32)


def _dot_nt(a, b):
    return lax.dot_general(a, b, (((1,), (1,)), ((), ())), preferred_element_type=F32)


def _dot_tn(a, b):
    return lax.dot_general(a, b, (((0,), (0,)), ((), ())), preferred_element_type=F32)


def _sigmoid(x):
    return 1.0 / (1.0 + jnp.exp(-x))


def _log_sigmoid(x):
    return jnp.minimum(x, 0.0) - jnp.log1p(jnp.exp(-jnp.abs(x)))


def _softcap(x):
    return GATE_SOFTCAP * jnp.tanh(x / GATE_SOFTCAP)


def _norm_modulate(x, g, sc, sh):
    r = lax.rsqrt(jnp.mean(x * x, axis=-1, keepdims=True) + EPS)
    return (x * r * g) * (1.0 + sc) + sh


def _ada_kernel(c_ref, w_ref, b_ref, o_ref):
    c = c_ref[...]
    cond = c * _sigmoid(c)
    c_hi = cond.astype(BF16)
    c_lo = (cond - c_hi.astype(F32)).astype(BF16)
    w = w_ref[...]
    w_hi = w.astype(BF16)
    w_lo = (w - w_hi.astype(F32)).astype(BF16)
    o_ref[...] = _dot(c_hi, w_hi) + (_dot(c_hi, w_lo) + _dot(c_lo, w_hi)) + b_ref[...]


def _ada_modulation(c, ada_w, ada_b):
    depth, d, n = ada_w.shape
    bsz = c.shape[0]
    tn = 1024
    return pl.pallas_call(
        _ada_kernel,
        grid=(depth, n // tn),
        in_specs=[pl.BlockSpec((bsz, d), lambda l, j: (0, 0)),
                  pl.BlockSpec((None, d, tn), lambda l, j: (l, 0, j)),
                  pl.BlockSpec((None, 1, tn), lambda l, j: (l, 0, j))],
        out_specs=pl.BlockSpec((None, bsz, tn), lambda l, j: (l, 0, j)),
        out_shape=jax.ShapeDtypeStruct((depth, bsz, n), F32),
        compiler_params=_params("arbitrary", "arbitrary"),
        name="ada_modulation",
    )(c, ada_w, ada_b.reshape(depth, 1, n))


def _mod_spec(layer, which, bsz, tiles_per_batch):
    return pl.BlockSpec((None, 1, D_MODEL),
                        lambda i, *_: ((layer * bsz + i // tiles_per_batch) * 6 + which, 0, 0))


def _proj_kernel(x_ref, g_ref, sc_ref, sh_ref, w_ref, wt_ref, p_ref, t_ref, h_s):
    @pl.when(pl.program_id(1) == 0)
    def _():
        h = _norm_modulate(x_ref[...], g_ref[...], sc_ref[...], sh_ref[...]).astype(BF16)
        h_s[...] = h
        t_ref[...] = _dot(h, wt_ref[...])

    p_ref[...] = _dot(h_s[...], w_ref[...]).astype(p_ref.dtype)


def _in_projection(x2d, norm_g, mod3, layer, bsz, w_main, w_tail, tm, tn):
    m, d = x2d.shape
    n = w_main.shape[1]
    tpb = (m // bsz) // tm
    return pl.pallas_call(
        _proj_kernel,
        grid=(m // tm, n // tn),
        in_specs=[pl.BlockSpec((tm, d), lambda i, j: (i, 0)),
                  pl.BlockSpec((1, d), lambda i, j: (0, 0)),
                  _mod_spec(layer, 1, bsz, tpb),
                  _mod_spec(layer, 0, bsz, tpb),
                  pl.BlockSpec((d, tn), lambda i, j: (0, j)),
                  pl.BlockSpec((d, LANES), lambda i, j: (0, 0))],
        out_specs=[pl.BlockSpec((tm, tn), lambda i, j: (i, j)),
                   pl.BlockSpec((tm, LANES), lambda i, j: (i, 0))],
        out_shape=[jax.ShapeDtypeStruct((m, n), BF16),
                   jax.ShapeDtypeStruct((m, LANES), F32)],
        scratch_shapes=[pltpu.VMEM((tm, d), BF16)],
        compiler_params=_params("arbitrary", "arbitrary"),
        name=f"in_projection_{layer}",
    )(x2d, norm_g.reshape(1, d), mod3, mod3, w_main, w_tail)


def _rope_kernel(cos_ref, sin_ref):
    s, half = cos_ref.shape
    pos = lax.broadcasted_iota(I32, (s, half), 0).astype(F32)
    idx = lax.broadcasted_iota(I32, (s, half), 1).astype(F32)
    inv = jnp.exp(idx * (-math.log(ROPE_BASE) / half))
    ang = pos * inv
    cos_ref[...] = jnp.cos(ang)
    sin_ref[...] = jnp.sin(ang)


def _rope_tables(s):
    half = HEAD_DIM // 2
    return pl.pallas_call(
        _rope_kernel,
        out_shape=[jax.ShapeDtypeStruct((s, half), F32)] * 2,
        name="rope_tables",
    )()


def _group_norm_gate(h, ng, gate):
    r = lax.rsqrt(jnp.mean(h * h, axis=-1, keepdims=True) + EPS)
    return (h * r * ng * gate).astype(BF16)


def _mlstm_kernel(gb_ref, q_ref, k_ref, v_ref, o_ref, cwq_ref, cwk_ref, gc_ref, gbrow_ref, gr_ref,
                  ng_ref, y_ref, q_s, k_s, c_s, n_s, m_s):
    hps = ng_ref.shape[0]
    h0 = pl.program_id(1) * hps
    s = q_ref.shape[0]
    rows = lax.broadcasted_iota(I32, (s, 1), 0)

    def conv_silu(x_ref, cw_ref):
        x = x_ref[...].astype(F32)
        acc = x * cw_ref[CONV_WIDTH - 1:CONV_WIDTH, :]
        for sft in range(1, CONV_WIDTH):
            xs = jnp.where(rows >= sft, pltpu.roll(x, sft, 0), 0.0)
            acc = acc + xs * cw_ref[CONV_WIDTH - 1 - sft:CONV_WIDTH - sft, :]
        return acc * _sigmoid(acc)

    q_s[...] = conv_silu(q_ref, cwq_ref).astype(BF16)
    k_s[...] = (conv_silu(k_ref, cwk_ref) * (HEAD_DIM ** -0.5)).astype(BF16)
    c_s[...] = jnp.zeros_like(c_s)
    n_s[...] = jnp.zeros_like(n_s)
    m_s[...] = jnp.zeros_like(m_s)

    lane = lax.broadcasted_iota(I32, (CHUNK, LANES), 1)
    ri = lax.broadcasted_iota(I32, (CHUNK, CHUNK), 0)
    ci = lax.broadcasted_iota(I32, (CHUNK, CHUNK), 1)
    tril = ri >= ci

    def head_chunk(c, st, gcap, j):
        h = h0 + j
        cols = slice(j * HEAD_DIM, (j + 1) * HEAD_DIM)
        li_col = jnp.sum(jnp.where(lane == h, gcap, 0.0), axis=1, keepdims=True)
        lf_col = jnp.sum(jnp.where(lane == h + N_GROUP_HEADS, _log_sigmoid(gcap), 0.0), axis=1, keepdims=True)
        li_row = _softcap(gr_ref[c, pl.ds(h, 1), :] + gb_ref[h])
        lf_row = _log_sigmoid(_softcap(gr_ref[c, pl.ds(h + N_GROUP_HEADS, 1), :] + gb_ref[h + N_GROUP_HEADS]))
        bcum_col = jnp.sum(jnp.where(tril, lf_row, 0.0), axis=1, keepdims=True)
        bcum_row = jnp.sum(jnp.where(ri <= ci, lf_col, 0.0), axis=0, keepdims=True)
        b_last = jnp.sum(lf_row, axis=1, keepdims=True)
        m_prev = m_s[j]

        dmat = jnp.where(tril, bcum_col - bcum_row + li_row, -jnp.inf)
        m_inter = bcum_col + m_prev
        m_row = jnp.maximum(jnp.max(dmat, axis=1, keepdims=True), m_inter)
        qc = q_s[pl.ds(st, CHUNK), cols]
        kc = k_s[pl.ds(st, CHUNK), cols]
        vc = v_ref[pl.ds(st, CHUNK), cols]
        sc = _dot_nt(qc, kc) * jnp.exp(dmat - m_row)
        inter = jnp.exp(m_inter - m_row)
        num = _dot(sc.astype(BF16), vc) + inter * _dot(qc, c_s[j].astype(BF16))
        den = (jnp.sum(sc, axis=1, keepdims=True)
               + inter * jnp.sum(qc.astype(F32) * n_s[j], axis=1, keepdims=True))
        h_out = num / jnp.maximum(jnp.abs(den), jnp.exp(-m_row))

        g_col = b_last - bcum_col + li_col
        m_new = jnp.maximum(b_last + m_prev, jnp.max(g_col, axis=0, keepdims=True))
        decay = jnp.exp(b_last + m_prev - m_new)
        kw = kc.astype(F32) * jnp.exp(g_col - m_new)
        c_s[j] = decay * c_s[j] + _dot_tn(kw.astype(BF16), vc)
        n_s[j] = decay * n_s[j] + jnp.sum(kw, axis=0, keepdims=True)
        m_s[j] = m_new

        gate = _sigmoid(o_ref[pl.ds(st, CHUNK), cols].astype(F32))
        y_ref[pl.ds(st, CHUNK), cols] = _group_norm_gate(h_out, ng_ref[j], gate)

    def body(c, carry):
        st = pl.multiple_of(c * CHUNK, CHUNK)
        gcap = _softcap(gc_ref[pl.ds(st, CHUNK), :] + gbrow_ref[...])
        for j in range(hps):
            head_chunk(c, st, gcap, j)
        return carry

    lax.fori_loop(0, s // CHUNK, body, 0)


def _mlstm_heads(p3, gates_col, gates_row, conv_w, gate_b, head_norm_g):
    bsz, s, _ = p3.shape
    nh = N_GROUP_HEADS
    hps = MLSTM_HEADS_PER_STEP
    nsteps = nh // hps
    w = hps * HEAD_DIM

    def col(group):
        return pl.BlockSpec((None, s, w), lambda b, h: (b, 0, group * nsteps + h))

    gb_row = jnp.zeros((1, LANES), F32).at[0, :2 * nh].set(gate_b)
    return pl.pallas_call(
        _mlstm_kernel,
        grid=(bsz, nsteps),
        in_specs=[pl.BlockSpec(memory_space=pltpu.SMEM),
                  col(0), col(1), col(2), col(3),
                  pl.BlockSpec((CONV_WIDTH, w), lambda b, h: (0, h)),
                  pl.BlockSpec((CONV_WIDTH, w), lambda b, h: (0, nsteps + h)),
                  pl.BlockSpec((None, s, LANES), lambda b, h: (b, 0, 0)),
                  pl.BlockSpec((1, LANES), lambda b, h: (0, 0)),
                  pl.BlockSpec((None, s // CHUNK, 2 * nh, CHUNK), lambda b, h: (b, 0, 0, 0)),
                  pl.BlockSpec((hps, 1, HEAD_DIM), lambda b, h: (h, 0, 0))],
        out_specs=pl.BlockSpec((None, s, w), lambda b, h: (b, 0, h)),
        out_shape=jax.ShapeDtypeStruct((bsz, s, GROUP_WIDTH), BF16),
        scratch_shapes=[pltpu.VMEM((s, w), BF16), pltpu.VMEM((s, w), BF16),
                        pltpu.VMEM((hps, HEAD_DIM, HEAD_DIM), F32), pltpu.VMEM((hps, 1, HEAD_DIM), F32),
                        pltpu.VMEM((hps, 1, 1), F32)],
        compiler_params=_params("arbitrary", "arbitrary"),
        name="mlstm_heads",
    )(gate_b, p3, p3, p3, p3, conv_w, conv_w, gates_col, gb_row, gates_row,
      head_norm_g.reshape(2 * nh, 1, HEAD_DIM))


def _ret_kernel(q_ref, k_ref, v_ref, g_ref, cos_ref, sin_ref, ng_ref, y_ref, q_s, k_s, r_s):
    hps = ng_ref.shape[0]
    h0 = pl.program_id(1) * hps
    s = q_ref.shape[0]
    half = HEAD_DIM // 2
    cos = cos_ref[...]
    sin = sin_ref[...]

    def rotate(x_ref, dst, scale):
        for j in range(hps):
            lo = slice(j * HEAD_DIM, j * HEAD_DIM + half)
            hi = slice(j * HEAD_DIM + half, (j + 1) * HEAD_DIM)
            x1 = x_ref[:, lo].astype(F32)
            x2 = x_ref[:, hi].astype(F32)
            dst[:, lo] = ((x1 * cos - x2 * sin) * scale).astype(BF16)
            dst[:, hi] = ((x2 * cos + x1 * sin) * scale).astype(BF16)

    rotate(q_ref, q_s, 1.0)
    rotate(k_ref, k_s, HEAD_DIM ** -0.5)
    r_s[...] = jnp.zeros_like(r_s)

    ri = lax.broadcasted_iota(I32, (CHUNK, CHUNK), 0)
    ci = lax.broadcasted_iota(I32, (CHUNK, CHUNK), 1)
    rel = (ri - ci).astype(F32)
    jcol = lax.broadcasted_iota(I32, (CHUNK, 1), 0).astype(F32)
    decays = []
    for j in range(hps):
        log_gamma = jnp.full((1, 1), math.log(1.0 - 2.0 ** -5.0), F32)
        for i in range(1, N_GROUP_HEADS):
            log_gamma = jnp.where(h0 + j == i, math.log(1.0 - 2.0 ** (-5.0 - i)), log_gamma)
        decays.append((jnp.where(rel >= 0, jnp.exp(jnp.maximum(rel, 0.0) * log_gamma), 0.0),
                       jnp.exp((jcol + 1.0) * log_gamma),
                       jnp.exp((CHUNK - 1.0 - jcol) * log_gamma),
                       jnp.exp(CHUNK * log_gamma)))

    def body(i, carry):
        for u in range(RET_CHUNK_UNROLL):
            st = pl.multiple_of((i * RET_CHUNK_UNROLL + u) * CHUNK, CHUNK)
            for j in range(hps):
                dmask, cross_decay, state_decay, chunk_decay = decays[j]
                cols = slice(j * HEAD_DIM, (j + 1) * HEAD_DIM)
                qc = q_s[pl.ds(st, CHUNK), cols]
                kc = k_s[pl.ds(st, CHUNK), cols]
                vc = v_ref[pl.ds(st, CHUNK), cols]
                inner = _dot((_dot_nt(qc, kc) * dmask).astype(BF16), vc)
                cross = _dot(qc, r_s[j].astype(BF16)) * cross_decay
                r_s[j] = chunk_decay * r_s[j] + _dot_tn((kc.astype(F32) * state_decay).astype(BF16), vc)
                g = g_ref[pl.ds(st, CHUNK), cols].astype(F32)
                y_ref[pl.ds(st, CHUNK), cols] = _group_norm_gate(inner + cross, ng_ref[j], g * _sigmoid(g))
        return carry

    lax.fori_loop(0, s // (CHUNK * RET_CHUNK_UNROLL), body, 0)


def _retention_heads(p3, cos, sin, head_norm_g):
    bsz, s, _ = p3.shape
    nh = N_GROUP_HEADS
    hps = RET_HEADS_PER_STEP
    nsteps = nh // hps
    w = hps * HEAD_DIM

    def col(group):
        return pl.BlockSpec((None, s, w), lambda b, h: (b, 0, group * nsteps + h))

    return pl.pallas_call(
        _ret_kernel,
        grid=(bsz, nsteps),
        in_specs=[col(4), col(5), col(6), col(7),
                  pl.BlockSpec((s, HEAD_DIM // 2), lambda b, h: (0, 0)),
                  pl.BlockSpec((s, HEAD_DIM // 2), lambda b, h: (0, 0)),
                  pl.BlockSpec((hps, 1, HEAD_DIM), lambda b, h: (nsteps + h, 0, 0))],
        out_specs=pl.BlockSpec((None, s, w), lambda b, h: (b, 0, h)),
        out_shape=jax.ShapeDtypeStruct((bsz, s, GROUP_WIDTH), BF16),
        scratch_shapes=[pltpu.VMEM((s, w), BF16), pltpu.VMEM((s, w), BF16),
                        pltpu.VMEM((hps, HEAD_DIM, HEAD_DIM), F32)],
        compiler_params=_params("arbitrary", "arbitrary"),
        name="retention_heads",
    )(p3, p3, p3, p3, cos, sin, head_norm_g.reshape(2 * nh, 1, HEAD_DIM))


def _order_key(x):
    bits = pltpu.bitcast(x, I32)
    return bits ^ ((bits >> 31) & 0x7FFFFFFF)


def _dsa_kernel(tab_ref, q_ref, iq_ref, iw_ref, k_ref, v_ref, ika_ref, ikb_ref, qg_ref, kg_ref, o_ref,
                kext_s, ikab_s, vaug_s, keyt_s, tb_s, wb_s, qall_s, iqall_s, acc_s, m_s, *, topk):
    b = pl.program_id(0)
    qq = pl.program_id(1)
    s = k_ref.shape[0]
    nkb = s // Q_BLOCK
    subs = range(DSA_SUBS)
    ri = lax.broadcasted_iota(I32, (Q_BLOCK, Q_BLOCK), 0)
    ci = lax.broadcasted_iota(I32, (Q_BLOCK, Q_BLOCK), 1)

    def qblock(sub):
        return DSA_SUBS * qq + sub

    @pl.when((b == 0) & (qq == 0))
    def _():
        buckets = []
        for back in range(2):
            dist = jnp.maximum(back * Q_BLOCK + ri - ci, 0)
            bucket = jnp.full(dist.shape, REL_BUCKETS // 2, I32)
            for thr in T5_THRESHOLDS:
                bucket = bucket + jnp.where(dist >= thr, 1, 0)
            buckets.append(jnp.where(dist < REL_BUCKETS // 2, dist, bucket))

        def head_body(hh, carry):
            far = tab_ref[REL_BUCKETS - 1, hh]
            tiles = []
            for bucket in buckets:
                tile = jnp.zeros(bucket.shape, F32)
                for bk in range(REL_BUCKETS):
                    tile = jnp.where(bucket == bk, (tab_ref[bk, hh] - far) * LOG2E, tile)
                tiles.append(tile)
            zeros = jnp.zeros((Q_BLOCK, Q_BLOCK), F32)
            for i, (first, second) in enumerate(((tiles[0], zeros), (tiles[1], tiles[0]), (zeros, tiles[1]))):
                tb_s[i, hh, :, :Q_BLOCK] = first
                tb_s[i, hh, :, Q_BLOCK:] = second
            return carry

        lax.fori_loop(0, N_ATTN_HEADS, head_body, 0)
        onehot = jnp.where(ri == ci, 1.0, 0.0).astype(BF16)
        for sub in subs:
            for hh in range(N_ATTN_HEADS):
                qall_s[sub, hh * Q_BLOCK:(hh + 1) * Q_BLOCK, ATTN_DIM:] = onehot

    @pl.when(qq == 0)
    def _():
        k = k_ref[...].astype(F32)
        r = lax.rsqrt(jnp.mean(k * k, axis=-1, keepdims=True) + EPS)
        kn = (k * r * kg_ref[...]).astype(BF16)
        for sub in subs:
            kext_s[sub, :, :ATTN_DIM] = kn
        for kt in range(nkb):
            ikab_s[kt, :Q_BLOCK, :] = ika_ref[kt * Q_BLOCK:(kt + 1) * Q_BLOCK, :]
            ikab_s[kt, Q_BLOCK:, :] = ikb_ref[kt * Q_BLOCK:(kt + 1) * Q_BLOCK, :]
        vaug_s[:, :ATTN_DIM] = v_ref[...]
        vaug_s[:, ATTN_DIM:] = jnp.ones((s, ATTN_DIM), BF16)

    ones = jnp.ones((ATTN_DIM, ATTN_DIM), BF16)
    pair_rows = N_IDX_HEADS // 2 * Q_BLOCK
    for sub in subs:
        rows = slice(sub * Q_BLOCK, (sub + 1) * Q_BLOCK)
        qf = jnp.concatenate([q_ref[rows, hh * ATTN_DIM:(hh + 1) * ATTN_DIM] for hh in range(N_ATTN_HEADS)],
                             axis=0).astype(F32)
        sq = qf * qf
        sq_hi = sq.astype(BF16)
        sq_lo = (sq - sq_hi.astype(F32)).astype(BF16)
        mean_sq = (_dot(sq_hi, ones) + _dot(sq_lo, ones)) * (1.0 / ATTN_DIM)
        qall_s[sub, :, :ATTN_DIM] = (qf * lax.rsqrt(mean_sq + EPS)
                                     * (qg_ref[...] * (ATTN_DIM ** -0.5 * LOG2E))).astype(BF16)
        for hp in range(N_IDX_HEADS // 2):
            iqall_s[sub * pair_rows + hp * Q_BLOCK:sub * pair_rows + (hp + 1) * Q_BLOCK, :] = (
                iq_ref[rows, hp * LANES:(hp + 1) * LANES])
        wv = iw_ref[rows, :] * ((N_IDX_HEADS ** -0.5) * (IDX_DIM ** -0.5))
        for hh in range(N_IDX_HEADS):
            wb_s[sub, hh] = jnp.broadcast_to(wv[:, hh:hh + 1], (Q_BLOCK, Q_BLOCK))

    n_tiles = qq + 1

    def score_tile(kt):
        for half in range(KEY_TILE // Q_BLOCK):
            kb = 2 * kt + half
            r = _dot_nt(iqall_s[...], ikab_s[kb])
            for sub in subs:
                acc = jnp.zeros((Q_BLOCK, Q_BLOCK), F32)
                for hp in range(N_IDX_HEADS // 2):
                    blk = r[sub * pair_rows + hp * Q_BLOCK:sub * pair_rows + (hp + 1) * Q_BLOCK]
                    acc = acc + jnp.maximum(blk[:, :Q_BLOCK], 0.0) * wb_s[sub, 2 * hp]
                    acc = acc + jnp.maximum(blk[:, Q_BLOCK:], 0.0) * wb_s[sub, 2 * hp + 1]
                causal_t = (kb * Q_BLOCK + ri) <= (qblock(sub) * Q_BLOCK + ci)
                keyt_s[sub, kt, half * Q_BLOCK:(half + 1) * Q_BLOCK, :] = jnp.where(
                    causal_t, _order_key(acc.T), INT_MIN)

    def score_pair(i, carry):
        score_tile(2 * i)
        score_tile(2 * i + 1)
        return carry

    lax.fori_loop(0, n_tiles // 2, score_pair, 0)

    @pl.when(n_tiles % 2 == 1)
    def _():
        score_tile(n_tiles - 1)

    def count(pred):
        def inner(kt, accs):
            out = []
            for sub in subs:
                hit = jnp.where(pred(keyt_s[sub, kt], sub), 1.0, 0.0)
                out.append(accs[sub] + jnp.sum(hit.reshape(KEY_TILE // COUNT_ROWS, COUNT_ROWS, Q_BLOCK), axis=0))
            return tuple(out)
        accs = lax.fori_loop(0, n_tiles, inner, tuple(jnp.zeros((COUNT_ROWS, Q_BLOCK), F32) for _ in subs))
        return tuple(jnp.sum(acc, axis=0, keepdims=True) for acc in accs)

    def count_ge(cands):
        return count(lambda keys, sub: keys >= cands[sub])

    zero = jnp.zeros((1, Q_BLOCK), I32)
    thr0 = tuple(jnp.where(n >= topk, zero, INT_MIN) for n in count_ge((zero,) * DSA_SUBS))

    def bit_body(i, thrs):
        cands = tuple(t + jnp.left_shift(jnp.int32(1), 30 - i) for t in thrs)
        counts = count_ge(cands)
        return tuple(jnp.where(counts[sub] >= topk, cands[sub], thrs[sub]) for sub in subs)

    thrs = lax.fori_loop(0, 31, bit_body, thr0)
    thrs = tuple(jnp.maximum(t, INT_MIN + 1) for t in thrs)

    tied_over = sum(jnp.where(n > topk, 1.0, 0.0) for n in count_ge(thrs))

    @pl.when(jnp.max(tied_over) > 0.0)
    def _():
        greater = count(lambda keys, sub: keys > thrs[sub])
        kr = lax.broadcasted_iota(I32, (KEY_TILE, KEY_TILE), 0)
        kc = lax.broadcasted_iota(I32, (KEY_TILE, KEY_TILE), 1)
        lower = jnp.where(kc <= kr, 1.0, 0.0).astype(BF16)

        def retire(kt, seen):
            out = []
            for sub in subs:
                keys = keyt_s[sub, kt]
                tied = keys == thrs[sub]
                prefix = _dot(lower, jnp.where(tied, 1.0, 0.0).astype(BF16)) + seen[sub]
                keyt_s[sub, kt] = jnp.where(tied & (prefix > topk - greater[sub]), INT_MIN, keys)
                out.append(prefix[KEY_TILE - 1:, :])
            return tuple(out)

        lax.fori_loop(0, n_tiles, retire, tuple(jnp.zeros((1, Q_BLOCK), F32) for _ in subs))

    acc_s[...] = jnp.zeros_like(acc_s)
    m_s[...] = jnp.full(m_s.shape, NEG_BIG, F32)

    def attn_tile(kt, biases):
        st = pl.multiple_of(kt * KEY_TILE, KEY_TILE)
        for sub in subs:
            kext_s[sub, pl.ds(st, KEY_TILE), ATTN_DIM:] = jnp.where(
                keyt_s[sub, kt] >= thrs[sub], 0.0, NEG_BIG).astype(BF16)
        for sub in subs:
            sc = _dot_nt(qall_s[sub], kext_s[sub, pl.ds(st, KEY_TILE), :])
            sc = sc.reshape(N_ATTN_HEADS, Q_BLOCK, KEY_TILE)
            if biases[sub] is not None:
                sc = sc + tb_s[biases[sub]]
            m_prev = m_s[sub]
            m_new = jnp.maximum(m_prev, jnp.max(sc, axis=-1, keepdims=True))
            alpha = jnp.exp2(m_prev - m_new).reshape(N_ATTN_HEADS * Q_BLOCK, ATTN_DIM)
            p = jnp.concatenate([jnp.exp2(sc[:, :, i * ATTN_DIM:(i + 1) * ATTN_DIM] - m_new)
                                 for i in range(KEY_TILE // ATTN_DIM)], axis=-1)
            p = p.reshape(N_ATTN_HEADS * Q_BLOCK, KEY_TILE)
            pv = _dot(p.astype(BF16), vaug_s[pl.ds(st, KEY_TILE), :])
            acc_s[sub, :, :ATTN_DIM] = alpha * acc_s[sub, :, :ATTN_DIM] + pv[:, :ATTN_DIM]
            acc_s[sub, :, ATTN_DIM:] = alpha * acc_s[sub, :, ATTN_DIM:] + pv[:, ATTN_DIM:]
            m_s[sub] = m_new

    def far_pair(i, carry):
        attn_tile(2 * i, (None,) * DSA_SUBS)
        attn_tile(2 * i + 1, (None,) * DSA_SUBS)
        return carry

    n_far = jnp.maximum(qq - 1, 0)
    lax.fori_loop(0, n_far // 2, far_pair, 0)

    @pl.when(n_far % 2 == 1)
    def _():
        attn_tile(n_far - 1, (None,) * DSA_SUBS)

    @pl.when(qq >= 1)
    def _():
        attn_tile(qq - 1, (2, None))
        attn_tile(qq, (0, 1))

    @pl.when(qq == 0)
    def _():
        attn_tile(0, (0, 1))

    for sub in subs:
        out = acc_s[sub, :, :ATTN_DIM] / acc_s[sub, :, ATTN_DIM:]
        for hh in range(N_ATTN_HEADS):
            o_ref[sub * Q_BLOCK:(sub + 1) * Q_BLOCK, hh * ATTN_DIM:(hh + 1) * ATTN_DIM] = (
                out[hh * Q_BLOCK:(hh + 1) * Q_BLOCK].astype(BF16))


def _dsa_attention(p3, iw3, q_norm_g, k_norm_g, rel_bias):
    bsz, s, _ = p3.shape
    nq = s // Q_BLOCK
    qw = N_ATTN_HEADS * ATTN_DIM
    iqw = N_IDX_HEADS * IDX_DIM
    small0 = (qw + iqw) // LANES
    step_rows = DSA_SUBS * Q_BLOCK

    def small(off):
        return pl.BlockSpec((None, s, LANES), lambda b, i: (b, 0, small0 + off))

    return pl.pallas_call(
        functools.partial(_dsa_kernel, topk=min(MAX_TOPK, s // 4)),
        grid=(bsz, nq // DSA_SUBS),
        in_specs=[pl.BlockSpec(memory_space=pltpu.SMEM),
                  pl.BlockSpec((None, step_rows, qw), lambda b, i: (b, i, 0)),
                  pl.BlockSpec((None, step_rows, iqw), lambda b, i: (b, i, qw // iqw)),
                  pl.BlockSpec((None, step_rows, LANES), lambda b, i: (b, i, 0)),
                  small(0), small(1), small(2), small(3),
                  pl.BlockSpec((1, ATTN_DIM), lambda b, i: (0, 0)),
                  pl.BlockSpec((1, ATTN_DIM), lambda b, i: (0, 0))],
        out_specs=pl.BlockSpec((None, step_rows, qw), lambda b, i: (b, i, 0)),
        out_shape=jax.ShapeDtypeStruct((bsz, s, qw), BF16),
        scratch_shapes=[pltpu.VMEM((DSA_SUBS, s, 2 * ATTN_DIM), BF16),
                        pltpu.VMEM((nq, 2 * Q_BLOCK, LANES), BF16),
                        pltpu.VMEM((s, 2 * ATTN_DIM), BF16),
                        pltpu.VMEM((DSA_SUBS, s // KEY_TILE, KEY_TILE, Q_BLOCK), I32),
                        pltpu.VMEM((3, N_ATTN_HEADS, Q_BLOCK, KEY_TILE), F32),
                        pltpu.VMEM((DSA_SUBS, N_IDX_HEADS, Q_BLOCK, Q_BLOCK), F32),
                        pltpu.VMEM((DSA_SUBS, N_ATTN_HEADS * Q_BLOCK, 2 * ATTN_DIM), BF16),
                        pltpu.VMEM((DSA_SUBS * N_IDX_HEADS // 2 * Q_BLOCK, LANES), BF16),
                        pltpu.VMEM((DSA_SUBS, N_ATTN_HEADS * Q_BLOCK, 2 * ATTN_DIM), F32),
                        pltpu.VMEM((DSA_SUBS, N_ATTN_HEADS, Q_BLOCK, ATTN_DIM), F32)],
        compiler_params=_params("arbitrary", "arbitrary"),
        name="dsa_attention",
    )(rel_bias, p3, p3, iw3, p3, p3, p3, p3, q_norm_g.reshape(1, ATTN_DIM), k_norm_g.reshape(1, ATTN_DIM))


def _outproj_kernel(ya_ref, yb_ref, x_ref, w_ref, g1_ref, n2_ref, sc2_ref, sh2_ref, x1_ref, h2_ref):
    half = ya_ref.shape[1]
    mix = _dot(ya_ref[...], w_ref[:half, :]) + _dot(yb_ref[...], w_ref[half:, :])
    x1 = x_ref[...] + g1_ref[...] * mix
    x1_ref[...] = x1
    h2_ref[...] = _norm_modulate(x1, n2_ref[...], sc2_ref[...], sh2_ref[...]).astype(BF16)


def _out_projection(ya, yb, yb_col, x2d, w_out, norm2_g, mod3, layer, bsz, tm):
    m, d = x2d.shape
    tpb = (m // bsz) // tm
    row = pl.BlockSpec((tm, d), lambda i: (i, 0))
    return pl.pallas_call(
        _outproj_kernel,
        grid=(m // tm,),
        in_specs=[pl.BlockSpec((tm, d // 2), lambda i: (i, 0)),
                  pl.BlockSpec((tm, d // 2), lambda i: (i, yb_col)),
                  row,
                  pl.BlockSpec((d, d), lambda i: (0, 0)),
                  _mod_spec(layer, 2, bsz, tpb),
                  pl.BlockSpec((1, d), lambda i: (0, 0)),
                  _mod_spec(layer, 4, bsz, tpb),
                  _mod_spec(layer, 3, bsz, tpb)],
        out_specs=[row, row],
        out_shape=[jax.ShapeDtypeStruct((m, d), F32), jax.ShapeDtypeStruct((m, d), BF16)],
        compiler_params=_params("arbitrary"),
        name=f"out_projection_{layer}",
    )(ya, yb, x2d, w_out, mod3, norm2_g.reshape(1, d), mod3, mod3)


def _mlp_kernel(h_ref, x1_hbm, w1_ref, w2_ref, g2_ref, o_ref, x1_s, x1_sem):
    i = pl.program_id(0)
    j = pl.program_id(1)
    tm = o_ref.shape[0]

    def residual_copy():
        return pltpu.make_async_copy(x1_hbm.at[pl.ds(i * tm, tm), :], x1_s, x1_sem)

    @pl.when(j == 0)
    def _():
        residual_copy().start()
        o_ref[...] = jnp.zeros_like(o_ref)

    a = jnp.square(jnp.maximum(_dot(h_ref[...], w1_ref[...]), 0.0)).astype(BF16)
    o_ref[...] += _dot(a, w2_ref[...])

    @pl.when(j == pl.num_programs(1) - 1)
    def _():
        residual_copy().wait()
        o_ref[...] = x1_s[...] + g2_ref[...] * o_ref[...]


def _mlp(h2, x1, w1, w2, mod3, layer, bsz, tm, tf):
    m, d = x1.shape
    ff = w1.shape[-1]
    tpb = (m // bsz) // tm
    row = pl.BlockSpec((tm, d), lambda i, j: (i, 0))
    return pl.pallas_call(
        _mlp_kernel,
        grid=(m // tm, ff // tf),
        in_specs=[row,
                  pl.BlockSpec(memory_space=pl.ANY),
                  pl.BlockSpec((None, d, tf), lambda i, j: (layer, 0, j)),
                  pl.BlockSpec((None, tf, d), lambda i, j: (layer, j, 0)),
                  _mod_spec(layer, 5, bsz, tpb)],
        out_specs=row,
        out_shape=jax.ShapeDtypeStruct((m, d), F32),
        scratch_shapes=[pltpu.VMEM((tm, d), F32), pltpu.SemaphoreType.DMA(())],
        compiler_params=_params("arbitrary", "arbitrary"),
        name=f"mlp_{layer}",
    )(h2, x1, w1, w2, mod3)


def _pad_cols(w, width):
    return jnp.pad(w, ((0, 0), (0, width - w.shape[1])))


def _even_weights(w_in):
    gw = GROUP_WIDTH
    g0 = 4 * gw
    g1 = g0 + 2 * N_GROUP_HEADS
    main = jnp.concatenate([w_in[:, :g0], w_in[:, g1:]], axis=1).astype(BF16)
    tail = _pad_cols(w_in[:, g0:g1], LANES).astype(BF16)
    return main, tail


def _odd_weights(w_in):
    qw = N_ATTN_HEADS * ATTN_DIM
    iqw = N_IDX_HEADS * IDX_DIM
    o_k, o_v, o_iq = qw, qw + ATTN_DIM, qw + 2 * ATTN_DIM
    o_ik = o_iq + iqw
    o_iw = o_ik + IDX_DIM
    w_ik = w_in[:, o_ik:o_iw]
    zeros = jnp.zeros_like(w_ik)
    main = jnp.concatenate([w_in[:, :qw], w_in[:, o_iq:o_ik], w_in[:, o_k:o_v], w_in[:, o_v:o_iq],
                            w_ik, zeros, zeros, w_ik], axis=1).astype(BF16)
    tail = _pad_cols(w_in[:, o_iw:], LANES).astype(BF16)
    return main, tail


def kernel(x, c, ada_w, ada_b, norm1_g, norm2_g, mlp_w1, mlp_w2, even_w_in, even_conv_w, even_gate_b,
           even_head_norm_g, even_w_out, odd_w_in, odd_q_norm_g, odd_k_norm_g, odd_w_out, rel_bias):
    bsz, s, d = x.shape
    depth = ada_w.shape[0]
    m = bsz * s
    tm = ROW_TILE
    mod3 = _ada_modulation(c, ada_w, ada_b).reshape(depth * bsz * 6, 1, d)
    cos, sin = _rope_tables(s)
    w1_all = mlp_w1.astype(BF16)
    w2_all = mlp_w2.astype(BF16)
    xc = x.reshape(m, d)
    for l in range(depth):
        e = l // 2
        if l % 2 == 0:
            w_main, w_tail = _even_weights(even_w_in[e])
            p, gates = _in_projection(xc, norm1_g[l], mod3, l, bsz, w_main, w_tail, PROJ_ROW_TILE, EVEN_COL_TILE)
            p3 = p.reshape(bsz, s, -1)
            gates_col = gates.reshape(bsz, s, LANES)
            gates_row = jnp.swapaxes(
                gates_col[:, :, :2 * N_GROUP_HEADS].reshape(bsz, s // CHUNK, CHUNK, 2 * N_GROUP_HEADS), 2, 3)
            ya = _mlstm_heads(p3, gates_col, gates_row, even_conv_w[e], even_gate_b[e], even_head_norm_g[e])
            yb = _retention_heads(p3, cos, sin, even_head_norm_g[e])
            ya, yb, yb_col = ya.reshape(m, d // 2), yb.reshape(m, d // 2), 0
            w_out = even_w_out[e]
        else:
            w_main, w_tail = _odd_weights(odd_w_in[e])
            p, iw = _in_projection(xc, norm1_g[l], mod3, l, bsz, w_main, w_tail, PROJ_ROW_TILE, ODD_COL_TILE)
            y = _dsa_attention(p.reshape(bsz, s, -1), iw.reshape(bsz, s, LANES),
                               odd_q_norm_g[e], odd_k_norm_g[e], rel_bias)
            ya = yb = y.reshape(m, d)
            yb_col = 1
            w_out = odd_w_out[e]
        x1, h2 = _out_projection(ya, yb, yb_col, xc, w_out.astype(BF16), norm2_g[l], mod3, l, bsz, tm)
        xc = _mlp(h2, x1, w1_all, w2_all, mod3, l, bsz, MLP_ROW_TILE, FF_TILE)
    return xc.reshape(bsz, s, d)
```
